```python
import math
import jax
import jax.numpy as jnp
from jax import lax
import numpy as np

D_MODEL = 1024
BATCH = 16
SEQ = 256
DEPTH = 2
DEC_BATCH = 4
DEC_SEQ = 4096
PAST_LEN = 256

GRID_W = 64
ROPE_THETA = 10000.0
EPS = 1e-6
CHUNK = 64
Q_BLOCK = 128
CONV_K = 5
FFN_CONV_K = 3

GDN_HEADS = 4
GDN_DK = 64
GDN_DV = 64
GDN_WIDTH = GDN_HEADS * GDN_DV

MLA_HEADS = 8
MLA_Q_LORA = 256
MLA_KV_LORA = 128
MLA_NOPE = 64
MLA_ROPE = 32
MLA_V = 64
MLA_WIDTH = MLA_HEADS * MLA_V

SSD_HEADS = 4
SSD_HEAD_DIM = 64
SSD_INNER = SSD_HEADS * SSD_HEAD_DIM
SSD_GROUPS = 2
SSD_STATE = 128
SSD_CONV_CH = SSD_INNER + 2 * SSD_GROUPS * SSD_STATE

MIX_WIDTH = GDN_WIDTH + MLA_WIDTH + SSD_INNER
D_FF = 128 * ((8 * D_MODEL // 3 + 127) // 128)

IN_SIZES = (3 * GDN_WIDTH, GDN_WIDTH, 2 * GDN_HEADS, 2 * GDN_HEADS,
            MLA_Q_LORA, MLA_KV_LORA, MLA_ROPE,
            SSD_INNER, SSD_CONV_CH, 2 * SSD_HEADS)
IN_COLS = sum(IN_SIZES)
IN_OFFSETS = tuple(int(o) for o in np.cumsum(IN_SIZES)[:-1])

kernel_name = 'hybrid_gdn_mla_ssd_prefix_diffusion_step'


def rmsnorm(x, g):
    xf = x.astype(jnp.float32)
    y = xf * lax.rsqrt(jnp.mean(jnp.square(xf), axis=-1, keepdims=True) + EPS)
    return (y * g.astype(jnp.float32)).astype(x.dtype)


def l2norm(x):
    xf = x.astype(jnp.float32)
    return (xf * lax.rsqrt(jnp.sum(xf * xf, axis=-1, keepdims=True) + EPS)).astype(x.dtype)


def dwconv(x, w):
    k = w.shape[0]
    return lax.conv_general_dilated(
        x, w[:, None, :].astype(x.dtype), window_strides=(1,),
        padding=[(k // 2, k // 2)], dimension_numbers=('NWC', 'WIO', 'NWC'),
        feature_group_count=x.shape[-1])


def axial_rope(n_tokens):
    rows = n_tokens // GRID_W
    r, col = jnp.meshgrid(jnp.arange(rows, dtype=jnp.float32),
                          jnp.arange(GRID_W, dtype=jnp.float32), indexing='ij')
    r, col = r.reshape(-1), col.reshape(-1)
    nf = MLA_ROPE // 4
    inv = ROPE_THETA ** (-jnp.arange(nf, dtype=jnp.float32) / nf)
    ang = jnp.concatenate([r[:, None] * inv, col[:, None] * inv], axis=-1)
    return jnp.cos(ang), jnp.sin(ang)


def apply_rope(x, cos, sin):
    x1, x2 = jnp.split(x.astype(jnp.float32), 2, axis=-1)
    return jnp.concatenate([x1 * cos - x2 * sin, x1 * sin + x2 * cos], axis=-1).astype(x.dtype)


def gated_delta_chunked(q, k, v, log_a, beta, s0):
    bsz, length, heads, _ = k.shape
    dv = v.shape[-1]
    n = length // CHUNK
    f32 = jnp.float32

    def chunks(t):
        return t.astype(f32).reshape(bsz, n, CHUNK, heads, t.shape[-1]).transpose(1, 0, 3, 2, 4)

    qc, kc, vc = chunks(q), chunks(k), chunks(v)
    g = jnp.cumsum(chunks(log_a[..., None])[..., 0], axis=-1)
    bc = chunks(beta[..., None])
    incl = jnp.tril(jnp.ones((CHUNK, CHUNK), bool))
    strict = jnp.tril(jnp.ones((CHUNK, CHUNK), bool), -1)
    diff = g[..., :, None] - g[..., None, :]
    decay = jnp.where(incl, jnp.exp(jnp.where(incl, diff, 0.0)), 0.0)
    kb = kc * bc
    lmat = jnp.where(strict, jnp.einsum('nbhid,nbhjd->nbhij', kb, kc) * decay, 0.0)
    eye = jnp.eye(CHUNK, dtype=f32)
    t_inv = lax.linalg.triangular_solve(eye + lmat, jnp.broadcast_to(eye, lmat.shape),
                                        left_side=True, lower=True)
    u = jnp.einsum('nbhij,nbhjd->nbhid', t_inv, vc * bc)
    w = jnp.einsum('nbhij,nbhjd->nbhid', t_inv, kb * jnp.exp(g)[..., None])
    attn = jnp.where(incl, jnp.einsum('nbhid,nbhjd->nbhij', qc, kc) * decay, 0.0)
    q_dec = qc * jnp.exp(g)[..., None]
    g_last = g[..., -1]
    k_dec = kc * jnp.exp(g_last[..., None] - g)[..., None]

    def step(s, inp):
        u_i, w_i, attn_i, q_i, k_i, gl_i = inp
        v_new = u_i - jnp.einsum('bhcd,bhde->bhce', w_i, s)
        o = jnp.einsum('bhcd,bhde->bhce', q_i, s) + jnp.einsum('bhij,bhje->bhie', attn_i, v_new)
        s = s * jnp.exp(gl_i)[..., None, None] + jnp.einsum('bhcd,bhce->bhde', k_i, v_new)
        return s, o

    s_fin, o = lax.scan(step, s0.astype(f32), (u, w, attn, q_dec, k_dec, g_last))
    o = o.transpose(1, 0, 3, 2, 4).reshape(bsz, length, heads, dv)
    return o.astype(v.dtype), s_fin.astype(s0.dtype)


def ssd_chunked(x, dt, a, bm, cm, s0):
    bsz, length, heads, p = x.shape
    n = length // CHUNK
    f32 = jnp.float32

    def chunks(t):
        return t.astype(f32).reshape(bsz, n, CHUNK, *t.shape[2:])

    dtc = chunks(dt)
    xc = chunks(x) * dtc[..., None]
    bc, cc = chunks(bm), chunks(cm)
    acum = jnp.cumsum(dtc * a.astype(f32), axis=2)
    incl = jnp.tril(jnp.ones((CHUNK, CHUNK), bool))[:, :, None]
    seg = acum[:, :, :, None, :] - acum[:, :, None, :, :]
    lmat = jnp.where(incl, jnp.exp(jnp.where(incl, seg, 0.0)), 0.0)
    scores = jnp.einsum('bclhn,bcshn->bclsh', cc, bc) * lmat
    y_diag = jnp.einsum('bclsh,bcshp->bclhp', scores, xc)
    decay_states = jnp.exp(acum[:, :, -1:, :] - acum)
    chunk_states = jnp.einsum('bclhn,bclh,bclhp->bchpn', bc, decay_states, xc)
    chunk_decay = jnp.exp(acum[:, :, -1, :])

    def step(s, inp):
        st_i, dec_i = inp
        return s * dec_i[..., None, None] + st_i, s

    s_fin, s_prev = lax.scan(step, s0.astype(f32),
                             (chunk_states.swapaxes(0, 1), chunk_decay.swapaxes(0, 1)))
    s_prev = s_prev.swapaxes(0, 1)
    y_off = jnp.einsum('bclhn,bchpn,bclh->bclhp', cc, s_prev, jnp.exp(acum))
    y = (y_diag + y_off).reshape(bsz, length, heads, p)
    return y.astype(x.dtype), s_fin.astype(s0.dtype)


def block_attention(q_nope, q_rope, k_nope, k_rope, v):
    bsz, lq, heads, _ = q_nope.shape
    nb = lq // Q_BLOCK
    scale = (MLA_NOPE + MLA_ROPE) ** -0.5

    def one_block(blk):
        qn, qr = blk
        s = jnp.einsum('bqhd,bkhd->bhqk', qn, k_nope) + jnp.einsum('bqhr,bkr->bhqk', qr, k_rope)
        pr = jax.nn.softmax(s.astype(jnp.float32) * scale, axis=-1).astype(v.dtype)
        return jnp.einsum('bhqk,bkhd->bqhd', pr, v)

    qn_b = q_nope.reshape(bsz, nb, Q_BLOCK, heads, q_nope.shape[-1]).swapaxes(0, 1)
    qr_b = q_rope.reshape(bsz, nb, Q_BLOCK, heads, q_rope.shape[-1]).swapaxes(0, 1)
    out = lax.map(one_block, (qn_b, qr_b))
    return out.swapaxes(0, 1).reshape(bsz, lq, heads, v.shape[-1])


def token_mixers(h, lp, rope, ctx):
    bsz, length, _ = h.shape
    (gdn_qkv, gdn_gate, gdn_a, gdn_b, mla_cq, mla_ckv, mla_kr,
     ssd_z, ssd_xbc, ssd_dt) = jnp.split(h @ lp['w_in'], IN_OFFSETS, axis=-1)
    if ctx is None:
        s_gdn = jnp.zeros((bsz, 2, GDN_HEADS, GDN_DK, GDN_DV), h.dtype)
        s_ssd = jnp.zeros((bsz, 2, SSD_HEADS, SSD_HEAD_DIM, SSD_STATE), h.dtype)
    else:
        ctx_ckv, ctx_kr, s_gdn, s_ssd = ctx

    qkv = jax.nn.silu(dwconv(gdn_qkv, lp['gdn_conv']))
    q, k, v = jnp.split(qkv, 3, axis=-1)
    q = l2norm(q.reshape(bsz, length, GDN_HEADS, GDN_DK)) * (GDN_DK ** -0.5)
    k = l2norm(k.reshape(bsz, length, GDN_HEADS, GDN_DK))
    v = v.reshape(bsz, length, GDN_HEADS, GDN_DV)
    log_a = -jnp.exp(lp['gdn_a_log']) * jax.nn.softplus(
        gdn_a.reshape(bsz, length, 2, GDN_HEADS) + lp['gdn_dt_bias'])
    beta = jax.nn.sigmoid(gdn_b.reshape(bsz, length, 2, GDN_HEADS))
    o_f, sg_f = gated_delta_chunked(q, k, v, log_a[:, :, 0], beta[:, :, 0], s_gdn[:, 0])
    o_b, sg_b = gated_delta_chunked(q[:, ::-1], k[:, ::-1], v[:, ::-1],
                                    log_a[:, ::-1, 1], beta[:, ::-1, 1], s_gdn[:, 1])
    o_gdn = rmsnorm(o_f + o_b[:, ::-1], lp['gdn_norm']) * jax.nn.silu(
        gdn_gate.reshape(bsz, length, GDN_HEADS, GDN_DV))

    qm = (rmsnorm(mla_cq, lp['mla_q_norm']) @ lp['mla_w_uq']).reshape(
        bsz, length, MLA_HEADS, MLA_NOPE + MLA_ROPE)
    q_nope, q_rope = qm[..., :MLA_NOPE], qm[..., MLA_NOPE:]
    ckv = rmsnorm(mla_ckv, lp['mla_kv_norm'])
    if ctx is None:
        ckv_all, kr_all = ckv, mla_kr
    else:
        cos, sin = rope
        q_rope = apply_rope(q_rope, cos[:, None], sin[:, None])
        ckv_all = jnp.concatenate([ckv, ctx_ckv], axis=1)
        kr_all = jnp.concatenate([apply_rope(mla_kr, cos, sin), ctx_kr], axis=1)
    kv = (ckv_all @ lp['mla_w_ukv']).reshape(bsz, -1, MLA_HEADS, MLA_NOPE + MLA_V)
    o_mla = block_attention(q_nope, q_rope, kv[..., :MLA_NOPE], kr_all, kv[..., MLA_NOPE:])

    xbc = jax.nn.silu(dwconv(ssd_xbc, lp['ssd_conv']) + lp['ssd_conv_b'])
    xs, bm, cm = jnp.split(xbc, (SSD_INNER, SSD_INNER + SSD_GROUPS * SSD_STATE), axis=-1)
    xs = xs.reshape(bsz, length, SSD_HEADS, SSD_HEAD_DIM)
    rep = SSD_HEADS // SSD_GROUPS
    bm = jnp.repeat(bm.reshape(bsz, length, SSD_GROUPS, SSD_STATE), rep, axis=2)
    cm = jnp.repeat(cm.reshape(bsz, length, SSD_GROUPS, SSD_STATE), rep, axis=2)
    dt = jax.nn.softplus(ssd_dt.reshape(bsz, length, 2, SSD_HEADS) + lp['ssd_dt_bias'])
    a = -jnp.exp(lp['ssd_a_log'])
    y_f, ss_f = ssd_chunked(xs, dt[:, :, 0], a[0], bm, cm, s_ssd[:, 0])
    y_b, ss_b = ssd_chunked(xs[:, ::-1], dt[:, ::-1, 1], a[1], bm[:, ::-1], cm[:, ::-1], s_ssd[:, 1])
    y = y_f + y_b[:, ::-1] + xs * lp['ssd_d'][:, None]
    o_ssd = rmsnorm(y.reshape(bsz, length, SSD_INNER) * jax.nn.silu(ssd_z), lp['ssd_norm'])

    mixed = jnp.concatenate([o_gdn.reshape(bsz, length, GDN_WIDTH),
                             o_mla.reshape(bsz, length, MLA_WIDTH), o_ssd], axis=-1) @ lp['w_out']
    if ctx is None:
        return mixed, (ckv, mla_kr, jnp.stack([sg_f, sg_b], axis=1), jnp.stack([ss_f, ss_b], axis=1))
    return mixed, None


def conv_ffn(h, w_up, conv_w, w_down):
    gu = dwconv(h @ w_up, conv_w)
    g, u = jnp.split(gu, 2, axis=-1)
    return (jax.nn.silu(g) * u) @ w_down


def trunk_layer(x, cond, lp, rope, ctx):
    mod = (jax.nn.silu(cond) @ lp['w_ada'] + lp['b_ada']).reshape(-1, 1, 6 * D_MODEL)
    sh1, sc1, g1, sh2, sc2, g2 = jnp.split(mod, 6, axis=-1)
    h = rmsnorm(x, lp['norm_mix_pre']) * (1.0 + sc1) + sh1
    mixed, ctx_out = token_mixers(h, lp, rope, ctx)
    x = x + g1 * rmsnorm(mixed, lp['norm_mix_post'])
    h = rmsnorm(x, lp['norm_ffn_pre']) * (1.0 + sc2) + sh2
    f = conv_ffn(h, lp['ffn_w_up'], lp['ffn_conv'], lp['ffn_w_down'])
    x = x + g2 * rmsnorm(f, lp['norm_ffn_post'])
    return x, ctx_out


def setup_inputs(seed: int = 0) -> dict:
    key = jax.random.key(seed)
    ks = jax.random.split(key, 33)
    f32 = jnp.float32

    def nrm(i, shape, scale):
        return scale * jax.random.normal(ks[i], shape, f32)

    def gain(i, shape):
        return 1.0 + nrm(i, shape, 0.05)

    def dt_bias(i, shape):
        dt = jnp.exp(jax.random.uniform(ks[i], shape, f32, math.log(1e-3), math.log(1e-1)))
        return dt + jnp.log(-jnp.expm1(-dt))

    def a_log(i, shape):
        return jnp.log(jax.random.uniform(ks[i], shape, f32, 1.0, 16.0))

    return {
        'x_prompt': nrm(0, (BATCH, SEQ, D_MODEL), 1.0),
        'x_sample': nrm(1, (DEC_BATCH, DEC_SEQ, D_MODEL), 1.0),
        'cache_mla_ckv': nrm(2, (DEC_BATCH, DEPTH, PAST_LEN, MLA_KV_LORA), 1.0),
        'cache_mla_krope': nrm(3, (DEC_BATCH, DEPTH, PAST_LEN, MLA_ROPE), 1.0),
        'state_gdn': nrm(4, (DEC_BATCH, DEPTH, 2, GDN_HEADS, GDN_DK, GDN_DV), 0.3),
        'state_ssd': nrm(5, (DEC_BATCH, DEPTH, 2, SSD_HEADS, SSD_HEAD_DIM, SSD_STATE), 0.3),
        'c': nrm(6, (DEC_BATCH, D_MODEL), 1.0),
        'c_ctx': nrm(7, (D_MODEL,), 1.0),
        'w_ada': nrm(8, (DEPTH, D_MODEL, 6 * D_MODEL), 0.5 * D_MODEL ** -0.5),
        'b_ada': nrm(9, (DEPTH, 6 * D_MODEL), 0.02),
        'norm_mix_pre': gain(10, (DEPTH, D_MODEL)),
        'norm_mix_post': gain(11, (DEPTH, D_MODEL)),
        'norm_ffn_pre': gain(12, (DEPTH, D_MODEL)),
        'norm_ffn_post': gain(13, (DEPTH, D_MODEL)),
        'w_in': nrm(14, (DEPTH, D_MODEL, IN_COLS), D_MODEL ** -0.5),
        'gdn_conv': nrm(15, (DEPTH, CONV_K, 3 * GDN_WIDTH), CONV_K ** -0.5),
        'gdn_a_log': a_log(16, (DEPTH, 2, GDN_HEADS)),
        'gdn_dt_bias': dt_bias(17, (DEPTH, 2, GDN_HEADS)),
        'gdn_norm': gain(18, (DEPTH, GDN_DV)),
        'mla_q_norm': gain(19, (DEPTH, MLA_Q_LORA)),
        'mla_w_uq': nrm(20, (DEPTH, MLA_Q_LORA, MLA_HEADS * (MLA_NOPE + MLA_ROPE)), MLA_Q_LORA ** -0.5),
        'mla_kv_norm': gain(21, (DEPTH, MLA_KV_LORA)),
        'mla_w_ukv': nrm(22, (DEPTH, MLA_KV_LORA, MLA_HEADS * (MLA_NOPE + MLA_V)), MLA_KV_LORA ** -0.5),
        'ssd_conv': nrm(23, (DEPTH, CONV_K, SSD_CONV_CH), CONV_K ** -0.5),
        'ssd_conv_b': nrm(24, (DEPTH, SSD_CONV_CH), 0.02),
        'ssd_a_log': a_log(25, (DEPTH, 2, SSD_HEADS)),
        'ssd_dt_bias': dt_bias(26, (DEPTH, 2, SSD_HEADS)),
        'ssd_d': gain(27, (DEPTH, SSD_HEADS)),
        'ssd_norm': gain(28, (DEPTH, SSD_INNER)),
        'w_out': nrm(29, (DEPTH, MIX_WIDTH, D_MODEL), MIX_WIDTH ** -0.5),
        'ffn_w_up': nrm(30, (DEPTH, D_MODEL, 2 * D_FF), D_MODEL ** -0.5),
        'ffn_conv': nrm(31, (DEPTH, FFN_CONV_K, 2 * D_FF), FFN_CONV_K ** -0.5),
        'ffn_w_down': nrm(32, (DEPTH, D_FF, D_MODEL), D_FF ** -0.5),
    }


def reference(x_prompt, x_sample, cache_mla_ckv, cache_mla_krope, state_gdn, state_ssd, c, c_ctx,
              w_ada, b_ada, norm_mix_pre, norm_mix_post, norm_ffn_pre, norm_ffn_post, w_in,
              gdn_conv, gdn_a_log, gdn_dt_bias, gdn_norm, mla_q_norm, mla_w_uq, mla_kv_norm,
              mla_w_ukv, ssd_conv, ssd_conv_b, ssd_a_log, ssd_dt_bias, ssd_d, ssd_norm, w_out,
              ffn_w_up, ffn_conv, ffn_w_down):
    rope = axial_rope(x_sample.shape[1])
    y_prompt, y_sample = x_prompt, x_sample
    ckv_l, kr_l, sg_l, ss_l = [], [], [], []
    for l in range(DEPTH):
        lp = {
            'w_ada': w_ada[l], 'b_ada': b_ada[l],
            'norm_mix_pre': norm_mix_pre[l], 'norm_mix_post': norm_mix_post[l],
            'norm_ffn_pre': norm_ffn_pre[l], 'norm_ffn_post': norm_ffn_post[l],
            'w_in': w_in[l],
            'gdn_conv': gdn_conv[l], 'gdn_a_log': gdn_a_log[l], 'gdn_dt_bias': gdn_dt_bias[l],
            'gdn_norm': gdn_norm[l],
            'mla_q_norm': mla_q_norm[l], 'mla_w_uq': mla_w_uq[l],
            'mla_kv_norm': mla_kv_norm[l], 'mla_w_ukv': mla_w_ukv[l],
            'ssd_conv': ssd_conv[l], 'ssd_conv_b': ssd_conv_b[l], 'ssd_a_log': ssd_a_log[l],
            'ssd_dt_bias': ssd_dt_bias[l], 'ssd_d': ssd_d[l], 'ssd_norm': ssd_norm[l],
            'w_out': w_out[l],
            'ffn_w_up': ffn_w_up[l], 'ffn_conv': ffn_conv[l], 'ffn_w_down': ffn_w_down[l],
        }
        y_prompt, (ckv, kr, sg, ss) = trunk_layer(y_prompt, c_ctx, lp, None, None)
        ckv_l.append(ckv)
        kr_l.append(kr)
        sg_l.append(sg)
        ss_l.append(ss)
        y_sample, _ = trunk_layer(
            y_sample, c, lp, rope,
            (cache_mla_ckv[:, l], cache_mla_krope[:, l], state_gdn[:, l], state_ssd[:, l]))
    new_mla_ckv = jnp.stack(ckv_l, axis=1)
    new_mla_krope = jnp.stack(kr_l, axis=1)
    new_state_gdn = jnp.stack(sg_l, axis=1)
    new_state_ssd = jnp.stack(ss_l, axis=1)
    return (y_prompt, y_sample, new_mla_ckv, new_mla_krope, new_state_gdn, new_state_ssd)
```

```python
import functools

import numpy as np
import jax
import jax.numpy as jnp
from jax import lax
from jax.experimental import pallas as pl
from jax.experimental.pallas import tpu as pltpu

F32 = jnp.float32
BF16 = jnp.bfloat16

D_MODEL = 1024
DEPTH = 2
GRID_W = 64
ROPE_THETA = 10000.0
EPS = 1e-6
CHUNK = 64
CONV_K = 5
FFN_CONV_K = 3

GDN_HEADS = 4
GDN_DK = 64
GDN_DV = 64
GDN_WIDTH = GDN_HEADS * GDN_DV

MLA_HEADS = 8
MLA_Q_LORA = 256
MLA_KV_LORA = 128
MLA_NOPE = 64
MLA_ROPE = 32
MLA_V = 64
MLA_WIDTH = MLA_HEADS * MLA_V

SSD_HEADS = 4
SSD_HEAD_DIM = 64
SSD_INNER = SSD_HEADS * SSD_HEAD_DIM
SSD_GROUPS = 2
SSD_STATE = 128
SSD_CONV_CH = SSD_INNER + 2 * SSD_GROUPS * SSD_STATE

D_FF = 128 * ((8 * D_MODEL // 3 + 127) // 128)

IN_SIZES = (3 * GDN_WIDTH, GDN_WIDTH, 2 * GDN_HEADS, 2 * GDN_HEADS,
            MLA_Q_LORA, MLA_KV_LORA, MLA_ROPE,
            SSD_INNER, SSD_CONV_CH, 2 * SSD_HEADS)
IN_OFFSETS = tuple(int(o) for o in np.cumsum((0,) + IN_SIZES)[:-1])

LANES = 128
SUBLANES = 8
VMEM_LIMIT_BYTES = 56 * 1024 * 1024

SM_KR = 0
SM_GA = MLA_ROPE
SM_GB = SM_GA + 2 * GDN_HEADS
SM_DT = SM_GB + 2 * GDN_HEADS
PC_QKV = 0
PC_GATE = PC_QKV + 3 * GDN_WIDTH
PC_CQ = PC_GATE + GDN_WIDTH
PC_CKV = PC_CQ + MLA_Q_LORA
PC_Z = PC_CKV + MLA_KV_LORA
PC_XBC = PC_Z + SSD_INNER
PC_SMALL = PC_XBC + SSD_CONV_CH
PC_TOTAL = PC_SMALL + LANES
HEAD_BLOCK = LANES
SCAN_BLOCK = 256
HALO = SUBLANES


def _bdot(a, b):
    return jnp.dot(a.astype(BF16), b.astype(BF16), preferred_element_type=F32)


def _bdot_nt(a, b):
    return lax.dot_general(a.astype(BF16), b.astype(BF16), (((1,), (1,)), ((), ())),
                           preferred_element_type=F32)


def _bdot_tn(a, b):
    return lax.dot_general(a.astype(BF16), b.astype(BF16), (((0,), (0,)), ((), ())),
                           preferred_element_type=F32)


def _fdot(a, b):
    return jnp.dot(a, b, precision=lax.Precision.HIGHEST, preferred_element_type=F32)


def _silu(x):
    return x * (1.0 / (1.0 + jnp.exp(-x)))


def _sigmoid(x):
    return 1.0 / (1.0 + jnp.exp(-x))


def _softplus(x):
    return jnp.maximum(x, 0.0) + jnp.log(1.0 + jnp.exp(-jnp.abs(x)))


def _rms(x, gain):
    ms = jnp.mean(x * x, axis=-1, keepdims=True)
    return x * lax.rsqrt(ms + EPS) * gain


def _params(*sem):
    return pltpu.CompilerParams(dimension_semantics=sem, vmem_limit_bytes=VMEM_LIMIT_BYTES)


def _const_spec(shape):
    nd = len(shape)
    return pl.BlockSpec(shape, lambda *_: (0,) * nd, pipeline_mode=pl.Buffered(1))


def _ada_kernel(cond_ref, w_ref, b_ref, o_ref):
    a = _silu(cond_ref[...])
    o_ref[...] = jnp.dot(a, w_ref[...], precision=lax.Precision.HIGHEST,
                         preferred_element_type=F32) + b_ref[...]


def _ada(cond8, w_ada, b_ada):
    n_col = 6 * D_MODEL // D_MODEL
    return pl.pallas_call(
        _ada_kernel,
        out_shape=jax.ShapeDtypeStruct((DEPTH, SUBLANES, 6 * D_MODEL), F32),
        grid=(DEPTH, n_col),
        in_specs=[pl.BlockSpec((SUBLANES, D_MODEL), lambda l, j: (0, 0)),
                  pl.BlockSpec((None, D_MODEL, D_MODEL), lambda l, j: (l, 0, j)),
                  pl.BlockSpec((None, 1, D_MODEL), lambda l, j: (l, 0, j))],
        out_specs=pl.BlockSpec((None, SUBLANES, D_MODEL), lambda l, j: (l, 0, j)),
        compiler_params=_params("arbitrary", "arbitrary"),
        name="ada",
    )(cond8, w_ada, b_ada.reshape(DEPTH, 1, 6 * D_MODEL))


def _rope_lanes(x, c, s_up, s_dn):
    half = MLA_ROPE // 2
    return (x * c + pltpu.roll(x, half, 1) * s_up + pltpu.roll(x, LANES - half, 1) * s_dn)


def _inproj_kernel(use_rope, *refs):
    if use_rope:
        (x_ref, mod_ref, gpre_ref, win_ref, qn_ref, wuq_ref, kvn_ref, wk_ref, wv_ref, pk_ref,
         rope_ref, qkv_o, gate_o, qm_o, ckv_o, kmat_o, vmat_o, z_o, xbc_o, small_o) = refs
    else:
        (x_ref, mod_ref, gpre_ref, win_ref, qn_ref, wuq_ref, kvn_ref, wk_ref, wv_ref, pk_ref,
         qkv_o, gate_o, qm_o, ckv_o, kmat_o, vmat_o, z_o, xbc_o, small_o) = refs
    x = x_ref[...]
    sh1 = mod_ref[:, 0:D_MODEL]
    sc1 = mod_ref[:, D_MODEL:2 * D_MODEL]
    h = _rms(x, gpre_ref[...]) * (1.0 + sc1) + sh1
    proj = _bdot(h, win_ref[...])
    qkv_o[...] = proj[:, PC_QKV:PC_GATE]
    gate_o[...] = proj[:, PC_GATE:PC_CQ]
    z_o[...] = proj[:, PC_Z:PC_XBC]
    xbc_o[...] = proj[:, PC_XBC:PC_SMALL]
    small = proj[:, PC_SMALL:PC_TOTAL]
    small_o[...] = small

    cq = _rms(proj[:, PC_CQ:PC_CKV], qn_ref[...])
    qm = _bdot(cq, wuq_ref[...]) * ((MLA_NOPE + MLA_ROPE) ** -0.5)
    ckv = _rms(proj[:, PC_CKV:PC_Z], kvn_ref[...])
    ckv_o[...] = ckv
    if use_rope:
        cq_t = rope_ref[:, 0:LANES]
        s1q_t = rope_ref[:, LANES:2 * LANES]
        s2q_t = rope_ref[:, 2 * LANES:3 * LANES]
        for hd in range(MLA_HEADS):
            blk = qm[:, hd * HEAD_BLOCK:(hd + 1) * HEAD_BLOCK]
            qm_o[:, hd * HEAD_BLOCK:(hd + 1) * HEAD_BLOCK] = _rope_lanes(
                blk, cq_t, s1q_t, s2q_t).astype(BF16)
        kr = _rope_lanes(small, rope_ref[:, 3 * LANES:4 * LANES],
                         rope_ref[:, 4 * LANES:5 * LANES], rope_ref[:, 5 * LANES:6 * LANES])
    else:
        qm_o[...] = qm.astype(BF16)
        kr = small
    kmat_o[...] = (_bdot(ckv, wk_ref[...]) + _bdot(kr, pk_ref[...])).astype(BF16)
    vmat_o[...] = _bdot(ckv, wv_ref[...]).astype(BF16)


def _inproj(x, mod, per_batch_mod, lw, rope_tab, tm):
    bsz, length, _ = x.shape
    use_rope = rope_tab is not None
    nb = length // tm
    tok = lambda w: pl.BlockSpec((None, tm, w), lambda b, i: (b, i, 0))
    mod_map = (lambda b, i: (b, 0, 0)) if per_batch_mod else (lambda b, i: (0, 0, 0))
    in_specs = [tok(D_MODEL),
                pl.BlockSpec((None, 1, 6 * D_MODEL), mod_map),
                _const_spec((1, D_MODEL)),
                _const_spec((D_MODEL, PC_TOTAL)),
                _const_spec((1, MLA_Q_LORA)),
                _const_spec((MLA_Q_LORA, MLA_HEADS * HEAD_BLOCK)),
                _const_spec((1, MLA_KV_LORA)),
                _const_spec((MLA_KV_LORA, MLA_HEADS * HEAD_BLOCK)),
                _const_spec((MLA_KV_LORA, MLA_WIDTH)),
                _const_spec((LANES, MLA_HEADS * HEAD_BLOCK))]
    args = [x, mod, lw['norm_mix_pre'], lw['w_in'], lw['mla_q_norm'], lw['w_uq'],
            lw['mla_kv_norm'], lw['w_k'], lw['w_v'], lw['p_k']]
    if use_rope:
        in_specs.append(pl.BlockSpec((tm, 6 * LANES), lambda b, i: (i, 0)))
        args.append(rope_tab)
    widths = [(3 * GDN_WIDTH, F32), (GDN_WIDTH, F32), (MLA_HEADS * HEAD_BLOCK, BF16),
              (MLA_KV_LORA, F32), (MLA_HEADS * HEAD_BLOCK, BF16), (MLA_WIDTH, BF16),
              (SSD_INNER, F32), (SSD_CONV_CH, F32), (LANES, F32)]
    return pl.pallas_call(
        functools.partial(_inproj_kernel, use_rope),
        out_shape=[jax.ShapeDtypeStruct((bsz, length, w), dt) for w, dt in widths],
        grid=(bsz, nb),
        in_specs=in_specs,
        out_specs=[tok(w) for w, _ in widths],
        compiler_params=_params("parallel", "parallel"),
        name="inproj",
    )(*args)


def _ctxkv_kernel(ckv_ref, kr_ref, wk_ref, wv_ref, pk_ref, kmat_o, vmat_o):
    ckv = ckv_ref[...]
    kmat_o[...] = (_bdot(ckv, wk_ref[...]) + _bdot(kr_ref[...], pk_ref[...])).astype(BF16)
    vmat_o[...] = _bdot(ckv, wv_ref[...]).astype(BF16)


def _ctxkv(ckv, kr_pad, lw):
    bsz, length, _ = ckv.shape
    tok = lambda w: pl.BlockSpec((None, length, w), lambda b: (b, 0, 0))
    return pl.pallas_call(
        _ctxkv_kernel,
        out_shape=[jax.ShapeDtypeStruct((bsz, length, MLA_HEADS * HEAD_BLOCK), BF16),
                   jax.ShapeDtypeStruct((bsz, length, MLA_WIDTH), BF16)],
        grid=(bsz,),
        in_specs=[tok(MLA_KV_LORA), tok(LANES),
                  _const_spec((MLA_KV_LORA, MLA_HEADS * HEAD_BLOCK)),
                  _const_spec((MLA_KV_LORA, MLA_WIDTH)),
                  _const_spec((LANES, MLA_HEADS * HEAD_BLOCK))],
        out_specs=[tok(MLA_HEADS * HEAD_BLOCK), tok(MLA_WIDTH)],
        compiler_params=_params("parallel"),
        name="ctxkv",
    )(ckv, kr_pad, lw['w_k'], lw['w_v'], lw['p_k'])


def _attn_kernel(n_seg, *refs):
    q_ref = refs[0]
    k_refs = refs[1:1 + n_seg]
    v_refs = refs[1 + n_seg:1 + 2 * n_seg]
    o_ref = refs[1 + 2 * n_seg]
    outs = []
    for hh in range(2):
        q = q_ref[:, hh * HEAD_BLOCK:(hh + 1) * HEAD_BLOCK]
        s = [lax.dot_general(q, k[:, hh * HEAD_BLOCK:(hh + 1) * HEAD_BLOCK],
                             (((1,), (1,)), ((), ())), preferred_element_type=F32)
             for k in k_refs]
        m = s[0].max(axis=-1, keepdims=True)
        for t in s[1:]:
            m = jnp.maximum(m, t.max(axis=-1, keepdims=True))
        acc = None
        den = None
        for t, v in zip(s, v_refs):
            p = jnp.exp(t - m)
            d = p.sum(axis=-1, keepdims=True)
            pv = jnp.dot(p.astype(BF16), v[:, hh * MLA_V:(hh + 1) * MLA_V],
                         preferred_element_type=F32)
            acc = pv if acc is None else acc + pv
            den = d if den is None else den + d
        outs.append(acc / den)
    o_ref[...] = jnp.concatenate(outs, axis=-1).astype(BF16)


def _attention(qm, kmats, vmats, tq):
    bsz, length, _ = qm.shape
    n_seg = len(kmats)
    n_pair = MLA_HEADS // 2
    in_specs = [pl.BlockSpec((None, tq, 2 * HEAD_BLOCK), lambda b, p, i: (b, i, p))]
    for k in kmats:
        in_specs.append(pl.BlockSpec((None, k.shape[1], 2 * HEAD_BLOCK), lambda b, p, i: (b, 0, p)))
    for v in vmats:
        in_specs.append(pl.BlockSpec((None, v.shape[1], 2 * MLA_V), lambda b, p, i: (b, 0, p)))
    return pl.pallas_call(
        functools.partial(_attn_kernel, n_seg),
        out_shape=jax.ShapeDtypeStruct((bsz, length, MLA_WIDTH), BF16),
        grid=(bsz, n_pair, length // tq),
        in_specs=in_specs,
        out_specs=pl.BlockSpec((None, tq, 2 * MLA_V), lambda b, p, i: (b, i, p)),
        compiler_params=_params("parallel", "parallel", "arbitrary"),
        name="attn",
    )(qm, *kmats, *vmats)


def _conv_tile(xe_ref, prev_ref, main_ref, next_ref, w_ref, blk, n_blk, tb):
    xe_ref[0:HALO, :] = jnp.where(blk > 0, prev_ref[...], 0.0)
    xe_ref[HALO:HALO + tb, :] = main_ref[...]
    xe_ref[HALO + tb:HALO + tb + HALO, :] = jnp.where(blk < n_blk - 1, next_ref[...], 0.0)
    xe = xe_ref[...]
    rows = tb + 2 * HALO
    acc = None
    for j in range(CONV_K):
        shift = (CONV_K // 2 - j) % rows
        tap = xe if shift == 0 else pltpu.roll(xe, shift, 0)
        term = tap * w_ref[j:j + 1, :]
        acc = term if acc is None else acc + term
    return acc[HALO:HALO + tb, :]


def _chunk_tri(tb, reverse):
    r = lax.broadcasted_iota(jnp.int32, (tb, tb), 0)
    c = lax.broadcasted_iota(jnp.int32, (tb, tb), 1)
    same = (r // CHUNK) == (c // CHUNK)
    order = (c >= r) if reverse else (c <= r)
    return jnp.where(same & order, 1.0, 0.0).astype(F32)


def _masks(reverse):
    r = lax.broadcasted_iota(jnp.int32, (CHUNK, CHUNK), 0)
    c = lax.broadcasted_iota(jnp.int32, (CHUNK, CHUNK), 1)
    if reverse:
        return c >= r, c > r
    return c <= r, c < r


def _scan_specs(bsz, length, width, tb, reverse):
    n_blk = length // tb
    per = tb // HALO
    n_halo = length // HALO
    blk_of = (lambda i: n_blk - 1 - i) if reverse else (lambda i: i)
    main = pl.BlockSpec((None, tb, width), lambda b, i: (b, blk_of(i), 0))
    prev = pl.BlockSpec((None, HALO, width),
                        lambda b, i: (b, jnp.maximum(blk_of(i) * per - 1, 0), 0))
    nxt = pl.BlockSpec((None, HALO, width),
                       lambda b, i: (b, jnp.minimum((blk_of(i) + 1) * per, n_halo - 1), 0))
    small = pl.BlockSpec((None, tb, LANES), lambda b, i: (b, blk_of(i), 0))
    return n_blk, blk_of, main, prev, nxt, small


def _gdn_kernel(reverse, tb, n_blk, main_ref, prev_ref, next_ref, small_ref, w_ref, prow_ref,
                s0_ref, o_ref, sfin_ref, xe_ref, q_s, k_s, v_s, g_s, b_s, state):
    i = pl.program_id(1)
    blk = (n_blk - 1 - i) if reverse else i
    d = 1 if reverse else 0

    @pl.when(i == 0)
    def _():
        state[...] = s0_ref[...]

    qkv = _silu(_conv_tile(xe_ref, prev_ref, main_ref, next_ref, w_ref, blk, n_blk, tb))
    for hd in range(GDN_HEADS):
        lo = hd * GDN_DK
        qh = qkv[:, lo:lo + GDN_DK]
        kh = qkv[:, GDN_WIDTH + lo:GDN_WIDTH + lo + GDN_DK]
        q_s[:, lo:lo + GDN_DK] = qh * lax.rsqrt(
            jnp.sum(qh * qh, axis=-1, keepdims=True) + EPS) * (GDN_DK ** -0.5)
        k_s[:, lo:lo + GDN_DK] = kh * lax.rsqrt(jnp.sum(kh * kh, axis=-1, keepdims=True) + EPS)
    v_s[...] = qkv[:, 2 * GDN_WIDTH:3 * GDN_WIDTH]

    small = small_ref[...]
    log_a = -jnp.exp(prow_ref[0:1, :]) * _softplus(small + prow_ref[1:2, :])
    g_s[...] = jnp.dot(_chunk_tri(tb, reverse), log_a, precision=lax.Precision.HIGHEST,
                       preferred_element_type=F32)
    b_s[...] = _sigmoid(small)

    incl, strict = _masks(reverse)
    eye = jnp.where(lax.broadcasted_iota(jnp.int32, (CHUNK, CHUNK), 0)
                    == lax.broadcasted_iota(jnp.int32, (CHUNK, CHUNK), 1), 1.0, 0.0).astype(F32)
    n_chunk = tb // CHUNK

    def chunk_body(ci, carry):
        cc = (n_chunk - 1 - ci) if reverse else ci
        r0 = pl.multiple_of(cc * CHUNK, CHUNK)
        rows = pl.ds(r0, CHUNK)
        gc = g_s[rows, :]
        bc = b_s[rows, :]
        for hd in range(GDN_HEADS):
            lo = hd * GDN_DK
            la = SM_GA + d * GDN_HEADS + hd
            lb = SM_GB + d * GDN_HEADS + hd
            g_col = gc[:, la:la + 1]
            g_row = jnp.sum(eye * g_col, axis=0, keepdims=True)
            beta = bc[:, lb:lb + 1]
            g_last = g_row[:, 0:1] if reverse else g_row[:, CHUNK - 1:CHUNK]
            decay = jnp.where(incl, jnp.exp(jnp.where(incl, g_col - g_row, 0.0)), 0.0)
            qh = q_s[rows, lo:lo + GDN_DK]
            kh = k_s[rows, lo:lo + GDN_DK]
            vh = v_s[rows, lo:lo + GDN_DV]
            kk = _bdot_nt(kh, kh)
            qk = _bdot_nt(qh, kh)
            m1 = -jnp.where(strict, kk * decay, 0.0) * beta
            t_inv = eye + m1
            mp = m1
            for _ in range(5):
                mp = _fdot(mp, mp)
                t_inv = t_inv + _fdot(t_inv, mp)
            eg = jnp.exp(g_col)
            uw = _bdot(t_inv, jnp.concatenate([vh * beta, kh * (beta * eg)], axis=1))
            u = uw[:, 0:GDN_DV]
            w = uw[:, GDN_DV:GDN_DV + GDN_DK]
            attn = jnp.where(incl, qk * decay, 0.0)
            s_h = state[hd]
            ws_qs = _bdot(jnp.concatenate([w, qh * eg], axis=0), s_h)
            v_new = u - ws_qs[0:CHUNK, :]
            o_ref[rows, lo:lo + GDN_DV] = ws_qs[CHUNK:2 * CHUNK, :] + _bdot(attn, v_new)
            k_dec = kh * jnp.exp(g_last - g_col)
            state[hd] = s_h * jnp.exp(g_last) + _bdot_tn(k_dec, v_new)
        return carry

    lax.fori_loop(0, n_chunk, chunk_body, 0)
    sfin_ref[...] = state[...]


def _gdn(qkv_pre, small, conv_w, prow, s0, reverse, tb):
    bsz, length, width = qkv_pre.shape
    n_blk, blk_of, main, prev, nxt, sm = _scan_specs(bsz, length, width, tb, reverse)
    st = pl.BlockSpec((None, GDN_HEADS, GDN_DK, GDN_DV), lambda b, i: (b, 0, 0, 0))
    return pl.pallas_call(
        functools.partial(_gdn_kernel, reverse, tb, n_blk),
        out_shape=[jax.ShapeDtypeStruct((bsz, length, GDN_WIDTH), F32),
                   jax.ShapeDtypeStruct((bsz, GDN_HEADS, GDN_DK, GDN_DV), F32)],
        grid=(bsz, n_blk),
        in_specs=[main, prev, nxt, sm, _const_spec((SUBLANES, width)),
                  _const_spec((SUBLANES, LANES)), st],
        out_specs=[pl.BlockSpec((None, tb, GDN_WIDTH), lambda b, i: (b, blk_of(i), 0)), st],
        scratch_shapes=[pltpu.VMEM((tb + 2 * HALO, width), F32),
                        pltpu.VMEM((tb, GDN_WIDTH), F32), pltpu.VMEM((tb, GDN_WIDTH), F32),
                        pltpu.VMEM((tb, GDN_WIDTH), F32),
                        pltpu.VMEM((tb, LANES), F32), pltpu.VMEM((tb, LANES), F32),
                        pltpu.VMEM((GDN_HEADS, GDN_DK, GDN_DV), F32)],
        compiler_params=_params("parallel", "arbitrary"),
        name="gdn_bwd" if reverse else "gdn_fwd",
    )(qkv_pre, qkv_pre, qkv_pre, small, conv_w, prow, s0)


def _ssd_kernel(reverse, tb, n_blk, main_ref, prev_ref, next_ref, small_ref, w_ref, cb_ref,
                prow_ref, dvec_ref, s0_ref, y_ref, sfin_ref, xe_ref, x_s, b_s, c_s, dt_s, a_s,
                state):
    i = pl.program_id(1)
    blk = (n_blk - 1 - i) if reverse else i
    d = 1 if reverse else 0

    @pl.when(i == 0)
    def _():
        state[...] = s0_ref[...]

    xbc = _silu(_conv_tile(xe_ref, prev_ref, main_ref, next_ref, w_ref, blk, n_blk, tb)
                + cb_ref[...])
    x_s[...] = xbc[:, 0:SSD_INNER]
    b_s[...] = xbc[:, SSD_INNER:SSD_INNER + SSD_GROUPS * SSD_STATE]
    c_s[...] = xbc[:, SSD_INNER + SSD_GROUPS * SSD_STATE:SSD_CONV_CH]
    dt = _softplus(small_ref[...] + prow_ref[1:2, :])
    dt_s[...] = dt
    a_s[...] = jnp.dot(_chunk_tri(tb, reverse), dt * (-jnp.exp(prow_ref[0:1, :])),
                       precision=lax.Precision.HIGHEST, preferred_element_type=F32)

    incl, _ = _masks(reverse)
    eye = jnp.where(lax.broadcasted_iota(jnp.int32, (CHUNK, CHUNK), 0)
                    == lax.broadcasted_iota(jnp.int32, (CHUNK, CHUNK), 1), 1.0, 0.0).astype(F32)
    n_chunk = tb // CHUNK
    rep = SSD_HEADS // SSD_GROUPS

    def chunk_body(ci, carry):
        cc = (n_chunk - 1 - ci) if reverse else ci
        r0 = pl.multiple_of(cc * CHUNK, CHUNK)
        rows = pl.ds(r0, CHUNK)
        ac = a_s[rows, :]
        dtc = dt_s[rows, :]
        bms = [b_s[rows, g * SSD_STATE:(g + 1) * SSD_STATE] for g in range(SSD_GROUPS)]
        cms = [c_s[rows, g * SSD_STATE:(g + 1) * SSD_STATE] for g in range(SSD_GROUPS)]
        cbs = [_bdot_nt(cms[g], bms[g]) for g in range(SSD_GROUPS)]
        for hd in range(SSD_HEADS):
            lo = hd * SSD_HEAD_DIM
            g = hd // rep
            li = SM_DT + d * SSD_HEADS + hd
            a_col = ac[:, li:li + 1]
            a_row = jnp.sum(eye * a_col, axis=0, keepdims=True)
            a_last = a_row[:, 0:1] if reverse else a_row[:, CHUNK - 1:CHUNK]
            lmat = jnp.where(incl, jnp.exp(jnp.where(incl, a_col - a_row, 0.0)), 0.0)
            xh = x_s[rows, lo:lo + SSD_HEAD_DIM]
            xdt = xh * dtc[:, li:li + 1]
            s_h = state[hd]
            y = _bdot(cbs[g] * lmat, xdt) + _bdot_nt(cms[g], s_h) * jnp.exp(a_col)
            if not reverse:
                y = y + xh * dvec_ref[:, lo:lo + SSD_HEAD_DIM]
            y_ref[rows, lo:lo + SSD_HEAD_DIM] = y
            state[hd] = s_h * jnp.exp(a_last) + _bdot_tn(xdt, bms[g] * jnp.exp(a_last - a_col))
        return carry

    lax.fori_loop(0, n_chunk, chunk_body, 0)
    sfin_ref[...] = state[...]


def _ssd(xbc_pre, small, conv_w, conv_b, prow, dvec, s0, reverse, tb):
    bsz, length, width = xbc_pre.shape
    n_blk, blk_of, main, prev, nxt, sm = _scan_specs(bsz, length, width, tb, reverse)
    st = pl.BlockSpec((None, SSD_HEADS, SSD_HEAD_DIM, SSD_STATE), lambda b, i: (b, 0, 0, 0))
    return pl.pallas_call(
        functools.partial(_ssd_kernel, reverse, tb, n_blk),
        out_shape=[jax.ShapeDtypeStruct((bsz, length, SSD_INNER), F32),
                   jax.ShapeDtypeStruct((bsz, SSD_HEADS, SSD_HEAD_DIM, SSD_STATE), F32)],
        grid=(bsz, n_blk),
        in_specs=[main, prev, nxt, sm, _const_spec((SUBLANES, width)), _const_spec((1, width)),
                  _const_spec((SUBLANES, LANES)), _const_spec((1, SSD_INNER)), st],
        out_specs=[pl.BlockSpec((None, tb, SSD_INNER), lambda b, i: (b, blk_of(i), 0)), st],
        scratch_shapes=[pltpu.VMEM((tb + 2 * HALO, width), F32),
                        pltpu.VMEM((tb, SSD_INNER), F32),
                        pltpu.VMEM((tb, SSD_GROUPS * SSD_STATE), F32),
                        pltpu.VMEM((tb, SSD_GROUPS * SSD_STATE), F32),
                        pltpu.VMEM((tb, LANES), F32), pltpu.VMEM((tb, LANES), F32),
                        pltpu.VMEM((SSD_HEADS, SSD_HEAD_DIM, SSD_STATE), F32)],
        compiler_params=_params("parallel", "arbitrary"),
        name="ssd_bwd" if reverse else "ssd_fwd",
    )(xbc_pre, xbc_pre, xbc_pre, small, conv_w, conv_b, prow, dvec, s0)


def _outproj_kernel(x_ref, mod_ref, of_ref, ob_ref, gate_ref, om_ref, yf_ref, yb_ref, z_ref,
                    gn_ref, sn_ref, wout_ref, gpost_ref, o_ref):
    og = of_ref[...] + ob_ref[...]
    parts = []
    for hd in range(GDN_HEADS):
        lo = hd * GDN_DV
        oh = og[:, lo:lo + GDN_DV]
        parts.append(oh * lax.rsqrt(jnp.mean(oh * oh, axis=-1, keepdims=True) + EPS))
    o_gdn = jnp.concatenate(parts, axis=-1) * gn_ref[...] * _silu(gate_ref[...])
    o_ssd = _rms((yf_ref[...] + yb_ref[...]) * _silu(z_ref[...]), sn_ref[...])
    mixed = (_bdot(o_gdn, wout_ref[0:GDN_WIDTH, :])
             + jnp.dot(om_ref[...], wout_ref[GDN_WIDTH:GDN_WIDTH + MLA_WIDTH, :],
                       preferred_element_type=F32)
             + _bdot(o_ssd, wout_ref[GDN_WIDTH + MLA_WIDTH:, :]))
    g1 = mod_ref[:, 2 * D_MODEL:3 * D_MODEL]
    o_ref[...] = x_ref[...] + g1 * _rms(mixed, gpost_ref[...])


def _outproj(x, mod, per_batch_mod, o_f, o_b, gate, o_mla, y_f, y_b, z, lw, tm):
    bsz, length, _ = x.shape
    tok = lambda w: pl.BlockSpec((None, tm, w), lambda b, i: (b, i, 0))
    mod_map = (lambda b, i: (b, 0, 0)) if per_batch_mod else (lambda b, i: (0, 0, 0))
    return pl.pallas_call(
        _outproj_kernel,
        out_shape=jax.ShapeDtypeStruct((bsz, length, D_MODEL), F32),
        grid=(bsz, length // tm),
        in_specs=[tok(D_MODEL), pl.BlockSpec((None, 1, 6 * D_MODEL), mod_map),
                  tok(GDN_WIDTH), tok(GDN_WIDTH), tok(GDN_WIDTH), tok(MLA_WIDTH),
                  tok(SSD_INNER), tok(SSD_INNER), tok(SSD_INNER),
                  _const_spec((1, GDN_WIDTH)), _const_spec((1, SSD_INNER)),
                  _const_spec((D_MODEL, D_MODEL)), _const_spec((1, D_MODEL))],
        out_specs=tok(D_MODEL),
        compiler_params=_params("parallel", "parallel"),
        name="outproj",
    )(x, mod, o_f, o_b, gate, o_mla, y_f, y_b, z, lw['gdn_norm'], lw['ssd_norm'], lw['w_out'],
      lw['norm_mix_post'])


FFN_COLS = 256


def _ffn_kernel(tm, n_blk, main_ref, prev_ref, next_ref, mod_ref, gpre_ref, wup_ref, cw_ref,
                wdn_ref, gpost_ref, o_ref, xe_ref):
    i = pl.program_id(1)
    rows = tm + 2 * HALO
    xe_ref[0:HALO, :] = prev_ref[...]
    xe_ref[HALO:HALO + tm, :] = main_ref[...]
    xe_ref[HALO + tm:rows, :] = next_ref[...]
    sh2 = mod_ref[:, 3 * D_MODEL:4 * D_MODEL]
    sc2 = mod_ref[:, 4 * D_MODEL:5 * D_MODEL]
    g2 = mod_ref[:, 5 * D_MODEL:6 * D_MODEL]
    h = _rms(xe_ref[...], gpre_ref[...]) * (1.0 + sc2) + sh2
    r = lax.broadcasted_iota(jnp.int32, (rows, 1), 0)
    valid = ((r >= HALO) | (i > 0)) & ((r < HALO + tm) | (i < n_blk - 1))
    hb = jnp.where(valid, h, 0.0).astype(BF16)

    def conv3(up, col0):
        acc = None
        for j in range(FFN_CONV_K):
            shift = (FFN_CONV_K // 2 - j) % rows
            tap = up if shift == 0 else pltpu.roll(up, shift, 0)
            term = tap * cw_ref[j:j + 1, col0:col0 + FFN_COLS]
            acc = term if acc is None else acc + term
        return acc[HALO:HALO + tm, :]

    acc = None
    for cb in range(D_FF // FFN_COLS):
        c0 = cb * FFN_COLS
        g_up = jnp.dot(hb, wup_ref[:, c0:c0 + FFN_COLS], preferred_element_type=F32)
        u_up = jnp.dot(hb, wup_ref[:, D_FF + c0:D_FF + c0 + FFN_COLS],
                       preferred_element_type=F32)
        act = _silu(conv3(g_up, c0)) * conv3(u_up, D_FF + c0)
        part = _bdot(act, wdn_ref[c0:c0 + FFN_COLS, :])
        acc = part if acc is None else acc + part
    o_ref[...] = main_ref[...] + g2 * _rms(acc, gpost_ref[...])


def _ffn(x, mod, per_batch_mod, lw, tm):
    bsz, length, _ = x.shape
    n_blk = length // tm
    per = tm // HALO
    n_halo = length // HALO
    mod_map = (lambda b, i: (b, 0, 0)) if per_batch_mod else (lambda b, i: (0, 0, 0))
    return pl.pallas_call(
        functools.partial(_ffn_kernel, tm, n_blk),
        out_shape=jax.ShapeDtypeStruct((bsz, length, D_MODEL), F32),
        grid=(bsz, n_blk),
        in_specs=[pl.BlockSpec((None, tm, D_MODEL), lambda b, i: (b, i, 0)),
                  pl.BlockSpec((None, HALO, D_MODEL),
                               lambda b, i: (b, jnp.maximum(i * per - 1, 0), 0)),
                  pl.BlockSpec((None, HALO, D_MODEL),
                               lambda b, i: (b, jnp.minimum((i + 1) * per, n_halo - 1), 0)),
                  pl.BlockSpec((None, 1, 6 * D_MODEL), mod_map),
                  _const_spec((1, D_MODEL)), _const_spec((D_MODEL, 2 * D_FF)),
                  _const_spec((SUBLANES, 2 * D_FF)), _const_spec((D_FF, D_MODEL)),
                  _const_spec((1, D_MODEL))],
        out_specs=pl.BlockSpec((None, tm, D_MODEL), lambda b, i: (b, i, 0)),
        scratch_shapes=[pltpu.VMEM((tm + 2 * HALO, D_MODEL), F32)],
        compiler_params=_params("parallel", "parallel"),
        name="ffn",
    )(x, x, x, mod, lw['norm_ffn_pre'], lw['ffn_w_up'], lw['ffn_conv'], lw['ffn_w_down'],
      lw['norm_ffn_post'])


def _pad_rows(w, rows):
    return jnp.pad(w, ((0, rows - w.shape[0]), (0, 0)))


def _lane_row(values, offset):
    return jnp.pad(values.reshape(1, -1), ((0, 0), (offset, LANES - offset - values.size)))


def _layer_weights(l, p):
    o = IN_OFFSETS
    w_in = p['w_in'][l]
    col = lambda k: w_in[:, o[k]:o[k] + IN_SIZES[k]]
    small = jnp.concatenate([col(6), col(2), col(3), col(9)], axis=1)
    small = jnp.pad(small, ((0, 0), (0, LANES - small.shape[1])))
    w_in_p = jnp.concatenate([col(0), col(1), col(4), col(5), col(7), col(8), small], axis=1)

    head_pad = HEAD_BLOCK - MLA_NOPE - MLA_ROPE
    w_uq = p['mla_w_uq'][l].reshape(MLA_Q_LORA, MLA_HEADS, MLA_NOPE + MLA_ROPE)
    w_uq = jnp.pad(w_uq, ((0, 0), (0, 0), (0, head_pad))).reshape(MLA_Q_LORA, -1)
    w_ukv = p['mla_w_ukv'][l].reshape(MLA_KV_LORA, MLA_HEADS, MLA_NOPE + MLA_V)
    w_k = jnp.pad(w_ukv[:, :, :MLA_NOPE], ((0, 0), (0, 0), (0, HEAD_BLOCK - MLA_NOPE)))
    w_k = w_k.reshape(MLA_KV_LORA, -1)
    w_v = w_ukv[:, :, MLA_NOPE:].reshape(MLA_KV_LORA, -1)
    place = jnp.pad(jnp.eye(MLA_ROPE, dtype=F32), ((0, LANES - MLA_ROPE), (MLA_NOPE, head_pad)))
    p_k = jnp.tile(place, (1, MLA_HEADS))

    gdn_prow = jnp.concatenate([_lane_row(p['gdn_a_log'][l], SM_GA),
                                _lane_row(p['gdn_dt_bias'][l], SM_GA)], axis=0)
    ssd_prow = jnp.concatenate([_lane_row(p['ssd_a_log'][l], SM_DT),
                                _lane_row(p['ssd_dt_bias'][l], SM_DT)], axis=0)
    row = lambda v: v.reshape(1, -1)
    return {
        'norm_mix_pre': row(p['norm_mix_pre'][l]), 'norm_mix_post': row(p['norm_mix_post'][l]),
        'norm_ffn_pre': row(p['norm_ffn_pre'][l]), 'norm_ffn_post': row(p['norm_ffn_post'][l]),
        'w_in': w_in_p.astype(BF16),
        'mla_q_norm': row(p['mla_q_norm'][l]), 'w_uq': w_uq.astype(BF16),
        'mla_kv_norm': row(p['mla_kv_norm'][l]), 'w_k': w_k.astype(BF16),
        'w_v': w_v.astype(BF16), 'p_k': p_k.astype(BF16),
        'gdn_conv': _pad_rows(p['gdn_conv'][l], SUBLANES), 'gdn_prow': _pad_rows(gdn_prow, SUBLANES),
        'gdn_norm': row(jnp.tile(p['gdn_norm'][l], GDN_HEADS)),
        'ssd_conv': _pad_rows(p['ssd_conv'][l], SUBLANES), 'ssd_conv_b': row(p['ssd_conv_b'][l]),
        'ssd_prow': _pad_rows(ssd_prow, SUBLANES),
        'ssd_d': row(jnp.repeat(p['ssd_d'][l], SSD_HEAD_DIM)),
        'ssd_norm': row(p['ssd_norm'][l]),
        'w_out': p['w_out'][l].astype(BF16),
        'ffn_w_up': p['ffn_w_up'][l].astype(BF16),
        'ffn_conv': _pad_rows(p['ffn_conv'][l], SUBLANES),
        'ffn_w_down': p['ffn_w_down'][l].astype(BF16),
    }


def _rope_table(n_tokens):
    t = jnp.arange(n_tokens, dtype=jnp.int32)
    r = (t // GRID_W).astype(F32)
    col = (t % GRID_W).astype(F32)
    nf = MLA_ROPE // 4
    half = MLA_ROPE // 2
    inv = ROPE_THETA ** (-jnp.arange(nf, dtype=F32) / nf)
    ang = jnp.concatenate([r[:, None] * inv, col[:, None] * inv], axis=-1)
    cos, sin = jnp.cos(ang), jnp.sin(ang)
    zeros = jnp.zeros_like(sin)

    def lanes(first, x1, x2, fill):
        out = jnp.full((n_tokens, LANES), fill, F32)
        out = out.at[:, first:first + half].set(x1)
        return out.at[:, first + half:first + 2 * half].set(x2)

    parts = []
    for first in (MLA_NOPE, SM_KR):
        parts += [lanes(first, cos, cos, 1.0), lanes(first, zeros, sin, 0.0),
                  lanes(first, -sin, zeros, 0.0)]
    return jnp.concatenate(parts, axis=-1)


def _trunk_layer(x, mod, per_batch_mod, lw, rope_tab, ctx, tm, tq):
    bsz, length, _ = x.shape
    tb = min(SCAN_BLOCK, length)
    (qkv_pre, gate, qm, ckv, kmat, vmat, z, xbc_pre, small) = _inproj(
        x, mod, per_batch_mod, lw, rope_tab, tm)
    if ctx is None:
        kmats, vmats = [kmat], [vmat]
        s_gdn = jnp.zeros((2, bsz, GDN_HEADS, GDN_DK, GDN_DV), F32)
        s_ssd = jnp.zeros((2, bsz, SSD_HEADS, SSD_HEAD_DIM, SSD_STATE), F32)
    else:
        ctx_ckv, ctx_kr, st_gdn, st_ssd = ctx
        kr_pad = jnp.pad(ctx_kr, ((0, 0), (0, 0), (SM_KR, LANES - SM_KR - MLA_ROPE)))
        k_ctx, v_ctx = _ctxkv(ctx_ckv, kr_pad, lw)
        kmats, vmats = [kmat, k_ctx], [vmat, v_ctx]
        s_gdn = jnp.moveaxis(st_gdn, 1, 0)
        s_ssd = jnp.moveaxis(st_ssd, 1, 0)
    o_f, sg_f = _gdn(qkv_pre, small, lw['gdn_conv'], lw['gdn_prow'], s_gdn[0], False, tb)
    o_b, sg_b = _gdn(qkv_pre, small, lw['gdn_conv'], lw['gdn_prow'], s_gdn[1], True, tb)
    y_f, ss_f = _ssd(xbc_pre, small, lw['ssd_conv'], lw['ssd_conv_b'], lw['ssd_prow'],
                     lw['ssd_d'], s_ssd[0], False, tb)
    y_b, ss_b = _ssd(xbc_pre, small, lw['ssd_conv'], lw['ssd_conv_b'], lw['ssd_prow'],
                     lw['ssd_d'], s_ssd[1], True, tb)
    o_mla = _attention(qm, kmats, vmats, tq)
    x = _outproj(x, mod, per_batch_mod, o_f, o_b, gate, o_mla, y_f, y_b, z, lw, tm)
    x = _ffn(x, mod, per_batch_mod, lw, tm)
    ctx_out = (ckv, small[:, :, SM_KR:SM_KR + MLA_ROPE], jnp.stack([sg_f, sg_b], axis=1),
               jnp.stack([ss_f, ss_b], axis=1))
    return x, ctx_out


def kernel(x_prompt, x_sample, cache_mla_ckv, cache_mla_krope, state_gdn, state_ssd, c, c_ctx, w_ada, b_ada, norm_mix_pre, norm_mix_post, norm_ffn_pre, norm_ffn_post, w_in, gdn_conv, gdn_a_log, gdn_dt_bias, gdn_norm, mla_q_norm, mla_w_uq, mla_kv_norm, mla_w_ukv, ssd_conv, ssd_conv_b, ssd_a_log, ssd_dt_bias, ssd_d, ssd_norm, w_out, ffn_w_up, ffn_conv, ffn_w_down):
    p = dict(norm_mix_pre=norm_mix_pre, norm_mix_post=norm_mix_post, norm_ffn_pre=norm_ffn_pre,
             norm_ffn_post=norm_ffn_post, w_in=w_in, gdn_conv=gdn_conv, gdn_a_log=gdn_a_log,
             gdn_dt_bias=gdn_dt_bias, gdn_norm=gdn_norm, mla_q_norm=mla_q_norm,
             mla_w_uq=mla_w_uq, mla_kv_norm=mla_kv_norm, mla_w_ukv=mla_w_ukv, ssd_conv=ssd_conv,
             ssd_conv_b=ssd_conv_b, ssd_a_log=ssd_a_log, ssd_dt_bias=ssd_dt_bias, ssd_d=ssd_d,
             ssd_norm=ssd_norm, w_out=w_out, ffn_w_up=ffn_w_up, ffn_conv=ffn_conv,
             ffn_w_down=ffn_w_down)
    dec_batch = x_sample.shape[0]
    cond8 = jnp.concatenate(
        [c, c_ctx[None, :], jnp.zeros((SUBLANES - dec_batch - 1, D_MODEL), F32)], axis=0)
    mod = _ada(cond8, w_ada, b_ada)
    rope_tab = _rope_table(x_sample.shape[1])

    y_prompt, y_sample = x_prompt, x_sample
    ckv_l, kr_l, sg_l, ss_l = [], [], [], []
    for l in range(DEPTH):
        lw = _layer_weights(l, p)
        mod_s = mod[l, 0:dec_batch].reshape(dec_batch, 1, 6 * D_MODEL)
        mod_p = mod[l, dec_batch:dec_batch + 1].reshape(1, 1, 6 * D_MODEL)
        y_prompt, (ckv, kr, sg, ss) = _trunk_layer(
            y_prompt, mod_p, False, lw, None, None, tm=256, tq=256)
        ckv_l.append(ckv)
        kr_l.append(kr)
        sg_l.append(sg)
        ss_l.append(ss)
        y_sample, _ = _trunk_layer(
            y_sample, mod_s, True, lw, rope_tab,
            (cache_mla_ckv[:, l], cache_mla_krope[:, l], state_gdn[:, l], state_ssd[:, l]),
            tm=512, tq=256)
    return (y_prompt, y_sample, jnp.stack(ckv_l, axis=1), jnp.stack(kr_l, axis=1),
            jnp.stack(sg_l, axis=1), jnp.stack(ss_l, axis=1))
```

```python
import functools

import numpy as np
import jax
import jax.numpy as jnp
from jax import lax
from jax.experimental import pallas as pl
from jax.experimental.pallas import tpu as pltpu

F32 = jnp.float32
BF16 = jnp.bfloat16

D_MODEL = 1024
DEPTH = 2
GRID_W = 64
ROPE_THETA = 10000.0
EPS = 1e-6
CHUNK = 64
CONV_K = 5
FFN_CONV_K = 3

GDN_HEADS = 4
GDN_DK = 64
GDN_DV = 64
GDN_WIDTH = GDN_HEADS * GDN_DV

MLA_HEADS = 8
MLA_Q_LORA = 256
MLA_KV_LORA = 128
MLA_NOPE = 64
MLA_ROPE = 32
MLA_V = 64
MLA_WIDTH = MLA_HEADS * MLA_V

SSD_HEADS = 4
SSD_HEAD_DIM = 64
SSD_INNER = SSD_HEADS * SSD_HEAD_DIM
SSD_GROUPS = 2
SSD_STATE = 128
SSD_CONV_CH = SSD_INNER + 2 * SSD_GROUPS * SSD_STATE

D_FF = 128 * ((8 * D_MODEL // 3 + 127) // 128)

IN_SIZES = (3 * GDN_WIDTH, GDN_WIDTH, 2 * GDN_HEADS, 2 * GDN_HEADS,
            MLA_Q_LORA, MLA_KV_LORA, MLA_ROPE,
            SSD_INNER, SSD_CONV_CH, 2 * SSD_HEADS)
IN_OFFSETS = tuple(int(o) for o in np.cumsum((0,) + IN_SIZES)[:-1])

LANES = 128
SUBLANES = 8
VMEM_LIMIT_BYTES = 56 * 1024 * 1024

SM_KR = 0
SM_GA = MLA_ROPE
SM_GB = SM_GA + 2 * GDN_HEADS
SM_DT = SM_GB + 2 * GDN_HEADS
PC_QKV = 0
PC_GATE = PC_QKV + 3 * GDN_WIDTH
PC_CQ = PC_GATE + GDN_WIDTH
PC_CKV = PC_CQ + MLA_Q_LORA
PC_Z = PC_CKV + MLA_KV_LORA
PC_XBC = PC_Z + SSD_INNER
PC_SMALL = PC_XBC + SSD_CONV_CH
PC_TOTAL = PC_SMALL + LANES
HEAD_BLOCK = LANES
SCAN_BLOCK = 256
HALO = SUBLANES


def _bdot(a, b):
    return jnp.dot(a.astype(BF16), b.astype(BF16), preferred_element_type=F32)


def _bdot_nt(a, b):
    return lax.dot_general(a.astype(BF16), b.astype(BF16), (((1,), (1,)), ((), ())),
                           preferred_element_type=F32)


def _bdot_tn(a, b):
    return lax.dot_general(a.astype(BF16), b.astype(BF16), (((0,), (0,)), ((), ())),
                           preferred_element_type=F32)


def _fdot(a, b):
    return jnp.dot(a, b, precision=lax.Precision.HIGHEST, preferred_element_type=F32)


def _silu(x):
    return x * (1.0 / (1.0 + jnp.exp(-x)))


def _sigmoid(x):
    return 1.0 / (1.0 + jnp.exp(-x))


def _softplus(x):
    return jnp.maximum(x, 0.0) + jnp.log(1.0 + jnp.exp(-jnp.abs(x)))


def _rms(x, gain):
    ms = jnp.mean(x * x, axis=-1, keepdims=True)
    return x * lax.rsqrt(ms + EPS) * gain


def _params(*sem):
    return pltpu.CompilerParams(dimension_semantics=sem, vmem_limit_bytes=VMEM_LIMIT_BYTES)


def _const_spec(shape):
    nd = len(shape)
    return pl.BlockSpec(shape, lambda *_: (0,) * nd, pipeline_mode=pl.Buffered(1))


def _ada_kernel(cond_ref, w_ref, b_ref, o_ref):
    a = _silu(cond_ref[...])
    o_ref[...] = jnp.dot(a, w_ref[...], precision=lax.Precision.HIGHEST,
                         preferred_element_type=F32) + b_ref[...]


def _ada(cond8, w_ada, b_ada):
    n_col = 6 * D_MODEL // D_MODEL
    return pl.pallas_call(
        _ada_kernel,
        out_shape=jax.ShapeDtypeStruct((DEPTH, SUBLANES, 6 * D_MODEL), F32),
        grid=(DEPTH, n_col),
        in_specs=[pl.BlockSpec((SUBLANES, D_MODEL), lambda l, j: (0, 0)),
                  pl.BlockSpec((None, D_MODEL, D_MODEL), lambda l, j: (l, 0, j)),
                  pl.BlockSpec((None, 1, D_MODEL), lambda l, j: (l, 0, j))],
        out_specs=pl.BlockSpec((None, SUBLANES, D_MODEL), lambda l, j: (l, 0, j)),
        compiler_params=_params("arbitrary", "arbitrary"),
        name="ada",
    )(cond8, w_ada, b_ada.reshape(DEPTH, 1, 6 * D_MODEL))


def _rope_lanes(x, c, s_up, s_dn):
    half = MLA_ROPE // 2
    return (x * c + pltpu.roll(x, half, 1) * s_up + pltpu.roll(x, LANES - half, 1) * s_dn)


def _inproj_kernel(use_rope, *refs):
    if use_rope:
        (x_ref, mod_ref, gpre_ref, win_ref, qn_ref, wuq_ref, kvn_ref, wk_ref, wv_ref, pk_ref,
         rope_ref, qkv_o, gate_o, qm_o, ckv_o, kmat_o, vmat_o, z_o, xbc_o, small_o) = refs
    else:
        (x_ref, mod_ref, gpre_ref, win_ref, qn_ref, wuq_ref, kvn_ref, wk_ref, wv_ref, pk_ref,
         qkv_o, gate_o, qm_o, ckv_o, kmat_o, vmat_o, z_o, xbc_o, small_o) = refs
    x = x_ref[...]
    sh1 = mod_ref[:, 0:D_MODEL]
    sc1 = mod_ref[:, D_MODEL:2 * D_MODEL]
    h = _rms(x, gpre_ref[...]) * (1.0 + sc1) + sh1
    proj = _bdot(h, win_ref[...])
    qkv_o[...] = proj[:, PC_QKV:PC_GATE]
    gate_o[...] = proj[:, PC_GATE:PC_CQ]
    z_o[...] = proj[:, PC_Z:PC_XBC]
    xbc_o[...] = proj[:, PC_XBC:PC_SMALL]
    small = proj[:, PC_SMALL:PC_TOTAL]
    small_o[...] = small

    cq = _rms(proj[:, PC_CQ:PC_CKV], qn_ref[...])
    qm = _bdot(cq, wuq_ref[...]) * ((MLA_NOPE + MLA_ROPE) ** -0.5)
    ckv = _rms(proj[:, PC_CKV:PC_Z], kvn_ref[...])
    ckv_o[...] = ckv
    if use_rope:
        cq_t = rope_ref[:, 0:LANES]
        s1q_t = rope_ref[:, LANES:2 * LANES]
        s2q_t = rope_ref[:, 2 * LANES:3 * LANES]
        for hd in range(MLA_HEADS):
            blk = qm[:, hd * HEAD_BLOCK:(hd + 1) * HEAD_BLOCK]
            qm_o[:, hd * HEAD_BLOCK:(hd + 1) * HEAD_BLOCK] = _rope_lanes(
                blk, cq_t, s1q_t, s2q_t).astype(BF16)
        kr = _rope_lanes(small, rope_ref[:, 3 * LANES:4 * LANES],
                         rope_ref[:, 4 * LANES:5 * LANES], rope_ref[:, 5 * LANES:6 * LANES])
    else:
        qm_o[...] = qm.astype(BF16)
        kr = small
    kmat_o[...] = (_bdot(ckv, wk_ref[...]) + _bdot(kr, pk_ref[...])).astype(BF16)
    vmat_o[...] = _bdot(ckv, wv_ref[...]).astype(BF16)


def _inproj(x, mod, per_batch_mod, lw, rope_tab, tm):
    bsz, length, _ = x.shape
    use_rope = rope_tab is not None
    nb = length // tm
    tok = lambda w: pl.BlockSpec((None, tm, w), lambda b, i: (b, i, 0))
    mod_map = (lambda b, i: (b, 0, 0)) if per_batch_mod else (lambda b, i: (0, 0, 0))
    in_specs = [tok(D_MODEL),
                pl.BlockSpec((None, 1, 6 * D_MODEL), mod_map),
                _const_spec((1, D_MODEL)),
                _const_spec((D_MODEL, PC_TOTAL)),
                _const_spec((1, MLA_Q_LORA)),
                _const_spec((MLA_Q_LORA, MLA_HEADS * HEAD_BLOCK)),
                _const_spec((1, MLA_KV_LORA)),
                _const_spec((MLA_KV_LORA, MLA_HEADS * HEAD_BLOCK)),
                _const_spec((MLA_KV_LORA, MLA_WIDTH)),
                _const_spec((LANES, MLA_HEADS * HEAD_BLOCK))]
    args = [x, mod, lw['norm_mix_pre'], lw['w_in'], lw['mla_q_norm'], lw['w_uq'],
            lw['mla_kv_norm'], lw['w_k'], lw['w_v'], lw['p_k']]
    if use_rope:
        in_specs.append(pl.BlockSpec((tm, 6 * LANES), lambda b, i: (i, 0)))
        args.append(rope_tab)
    widths = [(3 * GDN_WIDTH, F32), (GDN_WIDTH, F32), (MLA_HEADS * HEAD_BLOCK, BF16),
              (MLA_KV_LORA, F32), (MLA_HEADS * HEAD_BLOCK, BF16), (MLA_WIDTH, BF16),
              (SSD_INNER, F32), (SSD_CONV_CH, F32), (LANES, F32)]
    return pl.pallas_call(
        functools.partial(_inproj_kernel, use_rope),
        out_shape=[jax.ShapeDtypeStruct((bsz, length, w), dt) for w, dt in widths],
        grid=(bsz, nb),
        in_specs=in_specs,
        out_specs=[tok(w) for w, _ in widths],
        compiler_params=_params("parallel", "parallel"),
        name="inproj",
    )(*args)


def _ctxkv_kernel(ckv_ref, kr_ref, wk_ref, wv_ref, pk_ref, kmat_o, vmat_o):
    ckv = ckv_ref[...]
    kmat_o[...] = (_bdot(ckv, wk_ref[...]) + _bdot(kr_ref[...], pk_ref[...])).astype(BF16)
    vmat_o[...] = _bdot(ckv, wv_ref[...]).astype(BF16)


def _ctxkv(ckv, kr_pad, lw):
    bsz, length, _ = ckv.shape
    tok = lambda w: pl.BlockSpec((None, length, w), lambda b: (b, 0, 0))
    return pl.pallas_call(
        _ctxkv_kernel,
        out_shape=[jax.ShapeDtypeStruct((bsz, length, MLA_HEADS * HEAD_BLOCK), BF16),
                   jax.ShapeDtypeStruct((bsz, length, MLA_WIDTH), BF16)],
        grid=(bsz,),
        in_specs=[tok(MLA_KV_LORA), tok(LANES),
                  _const_spec((MLA_KV_LORA, MLA_HEADS * HEAD_BLOCK)),
                  _const_spec((MLA_KV_LORA, MLA_WIDTH)),
                  _const_spec((LANES, MLA_HEADS * HEAD_BLOCK))],
        out_specs=[tok(MLA_HEADS * HEAD_BLOCK), tok(MLA_WIDTH)],
        compiler_params=_params("parallel"),
        name="ctxkv",
    )(ckv, kr_pad, lw['w_k'], lw['w_v'], lw['p_k'])


def _attn_kernel(n_seg, *refs):
    q_ref = refs[0]
    k_refs = refs[1:1 + n_seg]
    v_refs = refs[1 + n_seg:1 + 2 * n_seg]
    o_ref = refs[1 + 2 * n_seg]
    outs = []
    for hh in range(2):
        q = q_ref[:, hh * HEAD_BLOCK:(hh + 1) * HEAD_BLOCK]
        s = [lax.dot_general(q, k[:, hh * HEAD_BLOCK:(hh + 1) * HEAD_BLOCK],
                             (((1,), (1,)), ((), ())), preferred_element_type=F32)
             for k in k_refs]
        m = s[0].max(axis=-1, keepdims=True)
        for t in s[1:]:
            m = jnp.maximum(m, t.max(axis=-1, keepdims=True))
        acc = None
        den = None
        for t, v in zip(s, v_refs):
            p = jnp.exp(t - m)
            d = p.sum(axis=-1, keepdims=True)
            pv = jnp.dot(p.astype(BF16), v[:, hh * MLA_V:(hh + 1) * MLA_V],
                         preferred_element_type=F32)
            acc = pv if acc is None else acc + pv
            den = d if den is None else den + d
        outs.append(acc / den)
    o_ref[...] = jnp.concatenate(outs, axis=-1).astype(BF16)


def _attention(qm, kmats, vmats, tq):
    bsz, length, _ = qm.shape
    n_seg = len(kmats)
    n_pair = MLA_HEADS // 2
    in_specs = [pl.BlockSpec((None, tq, 2 * HEAD_BLOCK), lambda b, p, i: (b, i, p))]
    for k in kmats:
        in_specs.append(pl.BlockSpec((None, k.shape[1], 2 * HEAD_BLOCK), lambda b, p, i: (b, 0, p)))
    for v in vmats:
        in_specs.append(pl.BlockSpec((None, v.shape[1], 2 * MLA_V), lambda b, p, i: (b, 0, p)))
    return pl.pallas_call(
        functools.partial(_attn_kernel, n_seg),
        out_shape=jax.ShapeDtypeStruct((bsz, length, MLA_WIDTH), BF16),
        grid=(bsz, n_pair, length // tq),
        in_specs=in_specs,
        out_specs=pl.BlockSpec((None, tq, 2 * MLA_V), lambda b, p, i: (b, i, p)),
        compiler_params=_params("parallel", "parallel", "arbitrary"),
        name="attn",
    )(qm, *kmats, *vmats)


def _conv_tile(xe_ref, prev_ref, main_ref, next_ref, w_ref, blk, n_blk, tb):
    xe_ref[0:HALO, :] = jnp.where(blk > 0, prev_ref[...], 0.0)
    xe_ref[HALO:HALO + tb, :] = main_ref[...]
    xe_ref[HALO + tb:HALO + tb + HALO, :] = jnp.where(blk < n_blk - 1, next_ref[...], 0.0)
    xe = xe_ref[...]
    rows = tb + 2 * HALO
    acc = None
    for j in range(CONV_K):
        shift = (CONV_K // 2 - j) % rows
        tap = xe if shift == 0 else pltpu.roll(xe, shift, 0)
        term = tap * w_ref[j:j + 1, :]
        acc = term if acc is None else acc + term
    return acc[HALO:HALO + tb, :]


def _chunk_tri(tb, reverse):
    r = lax.broadcasted_iota(jnp.int32, (tb, tb), 0)
    c = lax.broadcasted_iota(jnp.int32, (tb, tb), 1)
    same = (r // CHUNK) == (c // CHUNK)
    order = (c >= r) if reverse else (c <= r)
    return jnp.where(same & order, 1.0, 0.0).astype(BF16)


def _masks(reverse, width):
    r = lax.broadcasted_iota(jnp.int32, (CHUNK, width), 0)
    c = lax.broadcasted_iota(jnp.int32, (CHUNK, width), 1) % CHUNK
    eye = jnp.where(c == r, 1.0, 0.0).astype(F32)
    if reverse:
        return c >= r, c > r, eye
    return c <= r, c < r, eye


def _lane_block(shape, per):
    return lax.broadcasted_iota(jnp.int32, shape, 1) // per


def _bd(x, n_blk):
    blk = _lane_block(x.shape, x.shape[1] // n_blk)
    return jnp.concatenate([jnp.where(blk == h, x, 0.0) for h in range(n_blk)], axis=0)


def _split2(x):
    hi = x.astype(BF16).astype(F32)
    return hi, x - hi


def _split3(x):
    hi, r1 = _split2(x)
    mid, lo = _split2(r1)
    return hi.astype(BF16), mid.astype(BF16), lo.astype(BF16)


def _sel_dot(x, sel):
    return jnp.dot(jnp.concatenate(_split3(x), axis=1), jnp.concatenate([sel, sel, sel], axis=0),
                   preferred_element_type=F32)


def _tri_dot(tri, x):
    return jnp.dot(jnp.concatenate([tri, tri, tri], axis=1), jnp.concatenate(_split3(x), axis=0),
                   preferred_element_type=F32)


def _lane_select(rows, first, width, per):
    r = lax.broadcasted_iota(jnp.int32, (rows, width), 0)
    c = lax.broadcasted_iota(jnp.int32, (rows, width), 1)
    return jnp.where(r == first + c // per, 1.0, 0.0).astype(BF16)


def _scan_specs(bsz, length, width, tb, reverse):
    n_blk = length // tb
    per = tb // HALO
    n_halo = length // HALO
    blk_of = (lambda i: n_blk - 1 - i) if reverse else (lambda i: i)
    main = pl.BlockSpec((None, tb, width), lambda b, i: (b, blk_of(i), 0))
    prev = pl.BlockSpec((None, HALO, width),
                        lambda b, i: (b, jnp.maximum(blk_of(i) * per - 1, 0), 0))
    nxt = pl.BlockSpec((None, HALO, width),
                       lambda b, i: (b, jnp.minimum((blk_of(i) + 1) * per, n_halo - 1), 0))
    small = pl.BlockSpec((None, tb, LANES), lambda b, i: (b, blk_of(i), 0))
    return n_blk, blk_of, main, prev, nxt, small


def _gdn_kernel(reverse, tb, n_blk, main_ref, prev_ref, next_ref, small_ref, w_ref, prow_ref,
                s0_ref, o_ref, sfin_ref, xe_ref, q_s, k_s, v_s, g_s, b_s, state):
    i = pl.program_id(1)
    blk = (n_blk - 1 - i) if reverse else i
    d = 1 if reverse else 0

    @pl.when(i == 0)
    def _():
        state[...] = s0_ref[...]

    nh = GDN_HEADS
    qkv = _silu(_conv_tile(xe_ref, prev_ref, main_ref, next_ref, w_ref, blk, n_blk, tb))
    q = qkv[:, 0:GDN_WIDTH]
    k = qkv[:, GDN_WIDTH:2 * GDN_WIDTH]
    head_ones = jnp.where(
        lax.broadcasted_iota(jnp.int32, (GDN_WIDTH, GDN_WIDTH), 0) // GDN_DK
        == _lane_block((GDN_WIDTH, GDN_WIDTH), GDN_DK), 1.0, 0.0).astype(BF16)
    q_s[...] = q * lax.rsqrt(_sel_dot(q * q, head_ones) + EPS) * (GDN_DK ** -0.5)
    k_s[...] = k * lax.rsqrt(_sel_dot(k * k, head_ones) + EPS)
    v_s[...] = qkv[:, 2 * GDN_WIDTH:3 * GDN_WIDTH]

    small = small_ref[...]
    log_a = -jnp.exp(prow_ref[0:1, :]) * _softplus(small + prow_ref[1:2, :])
    packed = jnp.where(lax.broadcasted_iota(jnp.int32, small.shape, 1) < SM_GB, log_a,
                       _sigmoid(small))
    sel = jnp.concatenate([_lane_select(LANES, SM_GA + d * nh, GDN_WIDTH, GDN_DK),
                           _lane_select(LANES, SM_GB + d * nh, GDN_WIDTH, GDN_DK)], axis=1)
    spread = _sel_dot(packed, sel)
    g_s[...] = _tri_dot(_chunk_tri(tb, reverse), spread[:, 0:GDN_WIDTH])
    b_s[...] = spread[:, GDN_WIDTH:2 * GDN_WIDTH]

    incl, strict, eye = _masks(reverse, GDN_WIDTH)
    n_chunk = tb // CHUNK
    last = 0 if reverse else CHUNK - 1
    chunks = range(n_chunk)
    row_sl = [slice(c * CHUNK, (c + 1) * CHUNK) for c in chunks]

    g_col = [g_s[r, :] for r in row_sl]
    beta = [b_s[r, :] for r in row_sl]
    g_last = [g[last:last + 1, :] for g in g_col]
    decay = []
    for g in g_col:
        g_row = jnp.sum(eye * g, axis=0, keepdims=True)
        decay.append(jnp.where(incl, jnp.exp(jnp.where(incl, g - g_row, 0.0)), 0.0))
    q4 = [q_s[r, :] for r in row_sl]
    k4 = [k_s[r, :] for r in row_sl]
    qk_kk = [_bdot_nt(jnp.concatenate([q4[c], k4[c]], axis=0), _bd(k4[c], nh)) for c in chunks]
    m_pow = [-jnp.where(strict, qk_kk[c][CHUNK:2 * CHUNK, :] * decay[c], 0.0) * beta[c]
             for c in chunks]
    t_inv = [eye + m for m in m_pow]
    n_sq = 6
    for step in range(n_sq):
        for c in chunks:
            m_hi, m_lo = _split2(m_pow[c])
            rhs = jnp.concatenate(
                [_bd(m_hi, nh).astype(BF16)] * 2 + [_bd(m_lo, nh).astype(BF16)], axis=0)
            if step == 0:
                lhs_f = [(m_hi, m_lo)]
            elif step == n_sq - 1:
                lhs_f = [_split2(t_inv[c])]
            else:
                lhs_f = [(m_hi, m_lo), _split2(t_inv[c])]
            hi = jnp.concatenate([p[0] for p in lhs_f], axis=0).astype(BF16)
            lo = jnp.concatenate([p[1] for p in lhs_f], axis=0).astype(BF16)
            prod = jnp.dot(jnp.concatenate([hi, lo, hi], axis=1), rhs, preferred_element_type=F32)
            if step == 0:
                m_pow[c] = prod
            elif step == n_sq - 1:
                t_inv[c] = t_inv[c] + prod
            else:
                t_inv[c] = t_inv[c] + prod[CHUNK:2 * CHUNK, :]
                m_pow[c] = prod[0:CHUNK, :]
    uw, attn, q_dec, k_dec = [], [], [], []
    for c in chunks:
        eg = jnp.exp(g_col[c])
        v4 = v_s[row_sl[c], :]
        uw.append(_bdot(t_inv[c], jnp.concatenate(
            [_bd(v4 * beta[c], nh), _bd(k4[c] * (beta[c] * eg), nh)], axis=1)))
        attn.append(jnp.where(incl, qk_kk[c][0:CHUNK, :] * decay[c], 0.0))
        q_dec.append(q4[c] * eg)
        k_dec.append(k4[c] * jnp.exp(g_last[c] - g_col[c]))

    blk_id = _lane_block((GDN_DK, GDN_WIDTH), GDN_DV)
    s4 = state[...]
    for ci in chunks:
        c = (n_chunk - 1 - ci) if reverse else ci
        ws_qs = _bdot(jnp.concatenate([uw[c][:, GDN_WIDTH:2 * GDN_WIDTH], q_dec[c]], axis=0),
                      _bd(s4, nh))
        v_new = uw[c][:, 0:GDN_WIDTH] - ws_qs[0:CHUNK, :]
        o_ref[row_sl[c], :] = ws_qs[CHUNK:2 * CHUNK, :] + _bdot(attn[c], _bd(v_new, nh))
        kv = _bdot_tn(k_dec[c], v_new)
        upd = None
        for hd in range(nh):
            part = jnp.where(blk_id == hd, kv[hd * GDN_DK:(hd + 1) * GDN_DK, :], 0.0)
            upd = part if upd is None else upd + part
        s4 = s4 * jnp.exp(g_last[c]) + upd
    state[...] = s4
    sfin_ref[...] = s4


def _gdn(qkv_pre, small, conv_w, prow, s0, reverse, tb):
    bsz, length, width = qkv_pre.shape
    n_blk, blk_of, main, prev, nxt, sm = _scan_specs(bsz, length, width, tb, reverse)
    st = pl.BlockSpec((None, GDN_DK, GDN_WIDTH), lambda b, i: (b, 0, 0))
    s0 = s0.transpose(0, 2, 1, 3).reshape(bsz, GDN_DK, GDN_WIDTH)
    o, s_fin = pl.pallas_call(
        functools.partial(_gdn_kernel, reverse, tb, n_blk),
        out_shape=[jax.ShapeDtypeStruct((bsz, length, GDN_WIDTH), F32),
                   jax.ShapeDtypeStruct((bsz, GDN_DK, GDN_WIDTH), F32)],
        grid=(bsz, n_blk),
        in_specs=[main, prev, nxt, sm, _const_spec((SUBLANES, width)),
                  _const_spec((SUBLANES, LANES)), st],
        out_specs=[pl.BlockSpec((None, tb, GDN_WIDTH), lambda b, i: (b, blk_of(i), 0)), st],
        scratch_shapes=[pltpu.VMEM((tb + 2 * HALO, width), F32),
                        pltpu.VMEM((tb, GDN_WIDTH), F32), pltpu.VMEM((tb, GDN_WIDTH), F32),
                        pltpu.VMEM((tb, GDN_WIDTH), F32),
                        pltpu.VMEM((tb, GDN_WIDTH), F32), pltpu.VMEM((tb, GDN_WIDTH), F32),
                        pltpu.VMEM((GDN_DK, GDN_WIDTH), F32)],
        compiler_params=_params("parallel", "arbitrary"),
        name="gdn_bwd" if reverse else "gdn_fwd",
    )(qkv_pre, qkv_pre, qkv_pre, small, conv_w, prow, s0)
    return o, s_fin.reshape(bsz, GDN_DK, GDN_HEADS, GDN_DV).transpose(0, 2, 1, 3)


def _ssd_kernel(reverse, tb, n_blk, main_ref, prev_ref, next_ref, small_ref, w_ref, cb_ref,
                prow_ref, dvec_ref, s0_ref, y_ref, sfin_ref, xe_ref, x_s, b_s, c_s, dt_s, a_s,
                state):
    i = pl.program_id(1)
    blk = (n_blk - 1 - i) if reverse else i
    d = 1 if reverse else 0

    @pl.when(i == 0)
    def _():
        state[...] = s0_ref[...]

    xbc = _silu(_conv_tile(xe_ref, prev_ref, main_ref, next_ref, w_ref, blk, n_blk, tb)
                + cb_ref[...])
    x_s[...] = xbc[:, 0:SSD_INNER]
    b_s[...] = xbc[:, SSD_INNER:SSD_INNER + SSD_GROUPS * SSD_STATE]
    c_s[...] = xbc[:, SSD_INNER + SSD_GROUPS * SSD_STATE:SSD_CONV_CH]
    dt = _softplus(small_ref[...] + prow_ref[1:2, :])
    sel = _lane_select(LANES, SM_DT + d * SSD_HEADS, SSD_INNER, SSD_HEAD_DIM)
    spread = _sel_dot(jnp.concatenate([dt * (-jnp.exp(prow_ref[0:1, :])), dt], axis=0), sel)
    a_s[...] = _tri_dot(_chunk_tri(tb, reverse), spread[0:tb, :])
    dt_s[...] = spread[tb:2 * tb, :]

    incl, _, eye = _masks(reverse, SSD_INNER)
    n_chunk = tb // CHUNK
    rep = SSD_HEADS // SSD_GROUPS
    last = 0 if reverse else CHUNK - 1
    grp_of_lane = _lane_block((CHUNK, SSD_GROUPS * SSD_STATE), SSD_STATE)
    grp_of_lane_s = _lane_block((SSD_STATE, SSD_INNER), rep * SSD_HEAD_DIM)
    for ci in range(n_chunk):
        cc = (n_chunk - 1 - ci) if reverse else ci
        rows = slice(cc * CHUNK, (cc + 1) * CHUNK)
        a_col = a_s[rows, :]
        a_row = jnp.sum(eye * a_col, axis=0, keepdims=True)
        a_last = a_col[last:last + 1, :]
        lmat = jnp.where(incl, jnp.exp(jnp.where(incl, a_col - a_row, 0.0)), 0.0)
        xs = x_s[rows, :]
        bm = b_s[rows, :]
        cm = c_s[rows, :]
        xdt = xs * dt_s[rows, :]
        b_heads = jnp.concatenate(
            [jnp.where(grp_of_lane == hd // rep, bm, 0.0) for hd in range(SSD_HEADS)], axis=0)
        scores = _bdot_nt(cm, b_heads) * lmat
        st = state[...]
        s_groups = jnp.concatenate(
            [jnp.where(grp_of_lane_s == g, st, 0.0) for g in range(SSD_GROUPS)], axis=0)
        y = _bdot(scores, _bd(xdt, SSD_HEADS)) + _bdot(cm, s_groups) * jnp.exp(a_col)
        if not reverse:
            y = y + xs * dvec_ref[...]
        y_ref[rows, :] = y
        cs = _bdot_tn(bm, xdt * jnp.exp(a_last - a_col))
        upd = jnp.where(grp_of_lane_s == 0, cs[0:SSD_STATE, :], cs[SSD_STATE:2 * SSD_STATE, :])
        state[...] = st * jnp.exp(a_last) + upd
    sfin_ref[...] = state[...]


def _ssd(xbc_pre, small, conv_w, conv_b, prow, dvec, s0, reverse, tb):
    bsz, length, width = xbc_pre.shape
    n_blk, blk_of, main, prev, nxt, sm = _scan_specs(bsz, length, width, tb, reverse)
    st = pl.BlockSpec((None, SSD_STATE, SSD_INNER), lambda b, i: (b, 0, 0))
    s0 = s0.transpose(0, 3, 1, 2).reshape(bsz, SSD_STATE, SSD_INNER)
    y, s_fin = pl.pallas_call(
        functools.partial(_ssd_kernel, reverse, tb, n_blk),
        out_shape=[jax.ShapeDtypeStruct((bsz, length, SSD_INNER), F32),
                   jax.ShapeDtypeStruct((bsz, SSD_STATE, SSD_INNER), F32)],
        grid=(bsz, n_blk),
        in_specs=[main, prev, nxt, sm, _const_spec((SUBLANES, width)), _const_spec((1, width)),
                  _const_spec((SUBLANES, LANES)), _const_spec((1, SSD_INNER)), st],
        out_specs=[pl.BlockSpec((None, tb, SSD_INNER), lambda b, i: (b, blk_of(i), 0)), st],
        scratch_shapes=[pltpu.VMEM((tb + 2 * HALO, width), F32),
                        pltpu.VMEM((tb, SSD_INNER), F32),
                        pltpu.VMEM((tb, SSD_GROUPS * SSD_STATE), F32),
                        pltpu.VMEM((tb, SSD_GROUPS * SSD_STATE), F32),
                        pltpu.VMEM((tb, SSD_INNER), F32), pltpu.VMEM((tb, SSD_INNER), F32),
                        pltpu.VMEM((SSD_STATE, SSD_INNER), F32)],
        compiler_params=_params("parallel", "arbitrary"),
        name="ssd_bwd" if reverse else "ssd_fwd",
    )(xbc_pre, xbc_pre, xbc_pre, small, conv_w, conv_b, prow, dvec, s0)
    return y, s_fin.reshape(bsz, SSD_STATE, SSD_HEADS, SSD_HEAD_DIM).transpose(0, 2, 3, 1)


def _outproj_kernel(x_ref, mod_ref, of_ref, ob_ref, gate_ref, om_ref, yf_ref, yb_ref, z_ref,
                    gn_ref, sn_ref, wout_ref, gpost_ref, o_ref):
    og = of_ref[...] + ob_ref[...]
    parts = []
    for hd in range(GDN_HEADS):
        lo = hd * GDN_DV
        oh = og[:, lo:lo + GDN_DV]
        parts.append(oh * lax.rsqrt(jnp.mean(oh * oh, axis=-1, keepdims=True) + EPS))
    o_gdn = jnp.concatenate(parts, axis=-1) * gn_ref[...] * _silu(gate_ref[...])
    o_ssd = _rms((yf_ref[...] + yb_ref[...]) * _silu(z_ref[...]), sn_ref[...])
    mixed = (_bdot(o_gdn, wout_ref[0:GDN_WIDTH, :])
             + jnp.dot(om_ref[...], wout_ref[GDN_WIDTH:GDN_WIDTH + MLA_WIDTH, :],
                       preferred_element_type=F32)
             + _bdot(o_ssd, wout_ref[GDN_WIDTH + MLA_WIDTH:, :]))
    g1 = mod_ref[:, 2 * D_MODEL:3 * D_MODEL]
    o_ref[...] = x_ref[...] + g1 * _rms(mixed, gpost_ref[...])


def _outproj(x, mod, per_batch_mod, o_f, o_b, gate, o_mla, y_f, y_b, z, lw, tm):
    bsz, length, _ = x.shape
    tok = lambda w: pl.BlockSpec((None, tm, w), lambda b, i: (b, i, 0))
    mod_map = (lambda b, i: (b, 0, 0)) if per_batch_mod else (lambda b, i: (0, 0, 0))
    return pl.pallas_call(
        _outproj_kernel,
        out_shape=jax.ShapeDtypeStruct((bsz, length, D_MODEL), F32),
        grid=(bsz, length // tm),
        in_specs=[tok(D_MODEL), pl.BlockSpec((None, 1, 6 * D_MODEL), mod_map),
                  tok(GDN_WIDTH), tok(GDN_WIDTH), tok(GDN_WIDTH), tok(MLA_WIDTH),
                  tok(SSD_INNER), tok(SSD_INNER), tok(SSD_INNER),
                  _const_spec((1, GDN_WIDTH)), _const_spec((1, SSD_INNER)),
                  _const_spec((D_MODEL, D_MODEL)), _const_spec((1, D_MODEL))],
        out_specs=tok(D_MODEL),
        compiler_params=_params("parallel", "parallel"),
        name="outproj",
    )(x, mod, o_f, o_b, gate, o_mla, y_f, y_b, z, lw['gdn_norm'], lw['ssd_norm'], lw['w_out'],
      lw['norm_mix_post'])


FFN_COLS = 256


def _ffn_kernel(tm, n_blk, main_ref, prev_ref, next_ref, mod_ref, gpre_ref, wup_ref, cw_ref,
                wdn_ref, gpost_ref, o_ref, xe_ref):
    i = pl.program_id(1)
    rows = tm + 2 * HALO
    xe_ref[0:HALO, :] = prev_ref[...]
    xe_ref[HALO:HALO + tm, :] = main_ref[...]
    xe_ref[HALO + tm:rows, :] = next_ref[...]
    sh2 = mod_ref[:, 3 * D_MODEL:4 * D_MODEL]
    sc2 = mod_ref[:, 4 * D_MODEL:5 * D_MODEL]
    g2 = mod_ref[:, 5 * D_MODEL:6 * D_MODEL]
    h = _rms(xe_ref[...], gpre_ref[...]) * (1.0 + sc2) + sh2
    r = lax.broadcasted_iota(jnp.int32, (rows, 1), 0)
    valid = ((r >= HALO) | (i > 0)) & ((r < HALO + tm) | (i < n_blk - 1))
    hb = jnp.where(valid, h, 0.0).astype(BF16)

    def conv3(up, col0):
        acc = None
        for j in range(FFN_CONV_K):
            shift = (FFN_CONV_K // 2 - j) % rows
            tap = up if shift == 0 else pltpu.roll(up, shift, 0)
            term = tap * cw_ref[j:j + 1, col0:col0 + FFN_COLS]
            acc = term if acc is None else acc + term
        return acc[HALO:HALO + tm, :]

    acc = None
    for cb in range(D_FF // FFN_COLS):
        c0 = cb * FFN_COLS
        g_up = jnp.dot(hb, wup_ref[:, c0:c0 + FFN_COLS], preferred_element_type=F32)
        u_up = jnp.dot(hb, wup_ref[:, D_FF + c0:D_FF + c0 + FFN_COLS],
                       preferred_element_type=F32)
        act = _silu(conv3(g_up, c0)) * conv3(u_up, D_FF + c0)
        part = _bdot(act, wdn_ref[c0:c0 + FFN_COLS, :])
        acc = part if acc is None else acc + part
    o_ref[...] = main_ref[...] + g2 * _rms(acc, gpost_ref[...])


def _ffn(x, mod, per_batch_mod, lw, tm):
    bsz, length, _ = x.shape
    n_blk = length // tm
    per = tm // HALO
    n_halo = length // HALO
    mod_map = (lambda b, i: (b, 0, 0)) if per_batch_mod else (lambda b, i: (0, 0, 0))
    return pl.pallas_call(
        functools.partial(_ffn_kernel, tm, n_blk),
        out_shape=jax.ShapeDtypeStruct((bsz, length, D_MODEL), F32),
        grid=(bsz, n_blk),
        in_specs=[pl.BlockSpec((None, tm, D_MODEL), lambda b, i: (b, i, 0)),
                  pl.BlockSpec((None, HALO, D_MODEL),
                               lambda b, i: (b, jnp.maximum(i * per - 1, 0), 0)),
                  pl.BlockSpec((None, HALO, D_MODEL),
                               lambda b, i: (b, jnp.minimum((i + 1) * per, n_halo - 1), 0)),
                  pl.BlockSpec((None, 1, 6 * D_MODEL), mod_map),
                  _const_spec((1, D_MODEL)), _const_spec((D_MODEL, 2 * D_FF)),
                  _const_spec((SUBLANES, 2 * D_FF)), _const_spec((D_FF, D_MODEL)),
                  _const_spec((1, D_MODEL))],
        out_specs=pl.BlockSpec((None, tm, D_MODEL), lambda b, i: (b, i, 0)),
        scratch_shapes=[pltpu.VMEM((tm + 2 * HALO, D_MODEL), F32)],
        compiler_params=_params("parallel", "parallel"),
        name="ffn",
    )(x, x, x, mod, lw['norm_ffn_pre'], lw['ffn_w_up'], lw['ffn_conv'], lw['ffn_w_down'],
      lw['norm_ffn_post'])


def _pad_rows(w, rows):
    return jnp.pad(w, ((0, rows - w.shape[0]), (0, 0)))


def _lane_row(values, offset):
    return jnp.pad(values.reshape(1, -1), ((0, 0), (offset, LANES - offset - values.size)))


def _layer_weights(l, p):
    o = IN_OFFSETS
    w_in = p['w_in'][l]
    col = lambda k: w_in[:, o[k]:o[k] + IN_SIZES[k]]
    small = jnp.concatenate([col(6), col(2), col(3), col(9)], axis=1)
    small = jnp.pad(small, ((0, 0), (0, LANES - small.shape[1])))
    w_in_p = jnp.concatenate([col(0), col(1), col(4), col(5), col(7), col(8), small], axis=1)

    head_pad = HEAD_BLOCK - MLA_NOPE - MLA_ROPE
    w_uq = p['mla_w_uq'][l].reshape(MLA_Q_LORA, MLA_HEADS, MLA_NOPE + MLA_ROPE)
    w_uq = jnp.pad(w_uq, ((0, 0), (0, 0), (0, head_pad))).reshape(MLA_Q_LORA, -1)
    w_ukv = p['mla_w_ukv'][l].reshape(MLA_KV_LORA, MLA_HEADS, MLA_NOPE + MLA_V)
    w_k = jnp.pad(w_ukv[:, :, :MLA_NOPE], ((0, 0), (0, 0), (0, HEAD_BLOCK - MLA_NOPE)))
    w_k = w_k.reshape(MLA_KV_LORA, -1)
    w_v = w_ukv[:, :, MLA_NOPE:].reshape(MLA_KV_LORA, -1)
    place = jnp.pad(jnp.eye(MLA_ROPE, dtype=F32), ((0, LANES - MLA_ROPE), (MLA_NOPE, head_pad)))
    p_k = jnp.tile(place, (1, MLA_HEADS))

    gdn_prow = jnp.concatenate([_lane_row(p['gdn_a_log'][l], SM_GA),
                                _lane_row(p['gdn_dt_bias'][l], SM_GA)], axis=0)
    ssd_prow = jnp.concatenate([_lane_row(p['ssd_a_log'][l], SM_DT),
                                _lane_row(p['ssd_dt_bias'][l], SM_DT)], axis=0)
    row = lambda v: v.reshape(1, -1)
    return {
        'norm_mix_pre': row(p['norm_mix_pre'][l]), 'norm_mix_post': row(p['norm_mix_post'][l]),
        'norm_ffn_pre': row(p['norm_ffn_pre'][l]), 'norm_ffn_post': row(p['norm_ffn_post'][l]),
        'w_in': w_in_p.astype(BF16),
        'mla_q_norm': row(p['mla_q_norm'][l]), 'w_uq': w_uq.astype(BF16),
        'mla_kv_norm': row(p['mla_kv_norm'][l]), 'w_k': w_k.astype(BF16),
        'w_v': w_v.astype(BF16), 'p_k': p_k.astype(BF16),
        'gdn_conv': _pad_rows(p['gdn_conv'][l], SUBLANES), 'gdn_prow': _pad_rows(gdn_prow, SUBLANES),
        'gdn_norm': row(jnp.tile(p['gdn_norm'][l], GDN_HEADS)),
        'ssd_conv': _pad_rows(p['ssd_conv'][l], SUBLANES), 'ssd_conv_b': row(p['ssd_conv_b'][l]),
        'ssd_prow': _pad_rows(ssd_prow, SUBLANES),
        'ssd_d': row(jnp.repeat(p['ssd_d'][l], SSD_HEAD_DIM)),
        'ssd_norm': row(p['ssd_norm'][l]),
        'w_out': p['w_out'][l].astype(BF16),
        'ffn_w_up': p['ffn_w_up'][l].astype(BF16),
        'ffn_conv': _pad_rows(p['ffn_conv'][l], SUBLANES),
        'ffn_w_down': p['ffn_w_down'][l].astype(BF16),
    }


def _rope_table(n_tokens):
    t = jnp.arange(n_tokens, dtype=jnp.int32)
    r = (t // GRID_W).astype(F32)
    col = (t % GRID_W).astype(F32)
    nf = MLA_ROPE // 4
    half = MLA_ROPE // 2
    inv = ROPE_THETA ** (-jnp.arange(nf, dtype=F32) / nf)
    ang = jnp.concatenate([r[:, None] * inv, col[:, None] * inv], axis=-1)
    cos, sin = jnp.cos(ang), jnp.sin(ang)
    zeros = jnp.zeros_like(sin)

    def lanes(first, x1, x2, fill):
        out = jnp.full((n_tokens, LANES), fill, F32)
        out = out.at[:, first:first + half].set(x1)
        return out.at[:, first + half:first + 2 * half].set(x2)

    parts = []
    for first in (MLA_NOPE, SM_KR):
        parts += [lanes(first, cos, cos, 1.0), lanes(first, zeros, sin, 0.0),
                  lanes(first, -sin, zeros, 0.0)]
    return jnp.concatenate(parts, axis=-1)


def _trunk_layer(x, mod, per_batch_mod, lw, rope_tab, ctx, tm, tq):
    bsz, length, _ = x.shape
    tb = min(SCAN_BLOCK, length)
    (qkv_pre, gate, qm, ckv, kmat, vmat, z, xbc_pre, small) = _inproj(
        x, mod, per_batch_mod, lw, rope_tab, tm)
    if ctx is None:
        kmats, vmats = [kmat], [vmat]
        s_gdn = jnp.zeros((2, bsz, GDN_HEADS, GDN_DK, GDN_DV), F32)
        s_ssd = jnp.zeros((2, bsz, SSD_HEADS, SSD_HEAD_DIM, SSD_STATE), F32)
    else:
        ctx_ckv, ctx_kr, st_gdn, st_ssd = ctx
        kr_pad = jnp.pad(ctx_kr, ((0, 0), (0, 0), (SM_KR, LANES - SM_KR - MLA_ROPE)))
        k_ctx, v_ctx = _ctxkv(ctx_ckv, kr_pad, lw)
        kmats, vmats = [kmat, k_ctx], [vmat, v_ctx]
        s_gdn = jnp.moveaxis(st_gdn, 1, 0)
        s_ssd = jnp.moveaxis(st_ssd, 1, 0)
    o_f, sg_f = _gdn(qkv_pre, small, lw['gdn_conv'], lw['gdn_prow'], s_gdn[0], False, tb)
    o_b, sg_b = _gdn(qkv_pre, small, lw['gdn_conv'], lw['gdn_prow'], s_gdn[1], True, tb)
    y_f, ss_f = _ssd(xbc_pre, small, lw['ssd_conv'], lw['ssd_conv_b'], lw['ssd_prow'],
                     lw['ssd_d'], s_ssd[0], False, tb)
    y_b, ss_b = _ssd(xbc_pre, small, lw['ssd_conv'], lw['ssd_conv_b'], lw['ssd_prow'],
                     lw['ssd_d'], s_ssd[1], True, tb)
    o_mla = _attention(qm, kmats, vmats, tq)
    x = _outproj(x, mod, per_batch_mod, o_f, o_b, gate, o_mla, y_f, y_b, z, lw, tm)
    x = _ffn(x, mod, per_batch_mod, lw, tm)
    ctx_out = (ckv, small[:, :, SM_KR:SM_KR + MLA_ROPE], jnp.stack([sg_f, sg_b], axis=1),
               jnp.stack([ss_f, ss_b], axis=1))
    return x, ctx_out


def kernel(x_prompt, x_sample, cache_mla_ckv, cache_mla_krope, state_gdn, state_ssd, c, c_ctx, w_ada, b_ada, norm_mix_pre, norm_mix_post, norm_ffn_pre, norm_ffn_post, w_in, gdn_conv, gdn_a_log, gdn_dt_bias, gdn_norm, mla_q_norm, mla_w_uq, mla_kv_norm, mla_w_ukv, ssd_conv, ssd_conv_b, ssd_a_log, ssd_dt_bias, ssd_d, ssd_norm, w_out, ffn_w_up, ffn_conv, ffn_w_down):
    p = dict(norm_mix_pre=norm_mix_pre, norm_mix_post=norm_mix_post, norm_ffn_pre=norm_ffn_pre,
             norm_ffn_post=norm_ffn_post, w_in=w_in, gdn_conv=gdn_conv, gdn_a_log=gdn_a_log,
             gdn_dt_bias=gdn_dt_bias, gdn_norm=gdn_norm, mla_q_norm=mla_q_norm,
             mla_w_uq=mla_w_uq, mla_kv_norm=mla_kv_norm, mla_w_ukv=mla_w_ukv, ssd_conv=ssd_conv,
             ssd_conv_b=ssd_conv_b, ssd_a_log=ssd_a_log, ssd_dt_bias=ssd_dt_bias, ssd_d=ssd_d,
             ssd_norm=ssd_norm, w_out=w_out, ffn_w_up=ffn_w_up, ffn_conv=ffn_conv,
             ffn_w_down=ffn_w_down)
    dec_batch = x_sample.shape[0]
    cond8 = jnp.concatenate(
        [c, c_ctx[None, :], jnp.zeros((SUBLANES - dec_batch - 1, D_MODEL), F32)], axis=0)
    mod = _ada(cond8, w_ada, b_ada)
    rope_tab = _rope_table(x_sample.shape[1])

    y_prompt, y_sample = x_prompt, x_sample
    ckv_l, kr_l, sg_l, ss_l = [], [], [], []
    for l in range(DEPTH):
        lw = _layer_weights(l, p)
        mod_s = mod[l, 0:dec_batch].reshape(dec_batch, 1, 6 * D_MODEL)
        mod_p = mod[l, dec_batch:dec_batch + 1].reshape(1, 1, 6 * D_MODEL)
        y_prompt, (ckv, kr, sg, ss) = _trunk_layer(
            y_prompt, mod_p, False, lw, None, None, tm=256, tq=256)
        ckv_l.append(ckv)
        kr_l.append(kr)
        sg_l.append(sg)
        ss_l.append(ss)
        y_sample, _ = _trunk_layer(
            y_sample, mod_s, True, lw, rope_tab,
            (cache_mla_ckv[:, l], cache_mla_krope[:, l], state_gdn[:, l], state_ssd[:, l]),
            tm=512, tq=256)
    return (y_prompt, y_sample, jnp.stack(ckv_l, axis=1), jnp.stack(kr_l, axis=1),
            jnp.stack(sg_l, axis=1), jnp.stack(ss_l, axis=1))
```

```python
import functools

import numpy as np
import jax
import jax.numpy as jnp
from jax import lax
from jax.experimental import pallas as pl
from jax.experimental.pallas import tpu as pltpu

F32 = jnp.float32
BF16 = jnp.bfloat16

D_MODEL = 1024
DEPTH = 2
GRID_W = 64
ROPE_THETA = 10000.0
EPS = 1e-6
LOG2_E = 1.4426950408889634
CHUNK = 64
CONV_K = 5
FFN_CONV_K = 3

GDN_HEADS = 4
GDN_DK = 64
GDN_DV = 64
GDN_WIDTH = GDN_HEADS * GDN_DV

MLA_HEADS = 8
MLA_Q_LORA = 256
MLA_KV_LORA = 128
MLA_NOPE = 64
MLA_ROPE = 32
MLA_V = 64
MLA_WIDTH = MLA_HEADS * MLA_V

SSD_HEADS = 4
SSD_HEAD_DIM = 64
SSD_INNER = SSD_HEADS * SSD_HEAD_DIM
SSD_GROUPS = 2
SSD_STATE = 128
SSD_CONV_CH = SSD_INNER + 2 * SSD_GROUPS * SSD_STATE

D_FF = 128 * ((8 * D_MODEL // 3 + 127) // 128)

IN_SIZES = (3 * GDN_WIDTH, GDN_WIDTH, 2 * GDN_HEADS, 2 * GDN_HEADS,
            MLA_Q_LORA, MLA_KV_LORA, MLA_ROPE,
            SSD_INNER, SSD_CONV_CH, 2 * SSD_HEADS)
IN_OFFSETS = tuple(int(o) for o in np.cumsum((0,) + IN_SIZES)[:-1])

LANES = 128
SUBLANES = 8
VMEM_LIMIT_BYTES = 56 * 1024 * 1024

SM_KR = 0
SM_GA = MLA_ROPE
SM_GB = SM_GA + 2 * GDN_HEADS
SM_DT = SM_GB + 2 * GDN_HEADS
PC_QKV = 0
PC_GATE = PC_QKV + 3 * GDN_WIDTH
PC_CQ = PC_GATE + GDN_WIDTH
PC_CKV = PC_CQ + MLA_Q_LORA
PC_Z = PC_CKV + MLA_KV_LORA
PC_XBC = PC_Z + SSD_INNER
PC_SMALL = PC_XBC + SSD_CONV_CH
PC_TOTAL = PC_SMALL + LANES
HEAD_BLOCK = LANES
SCAN_BLOCK = 256
HALO = SUBLANES


def _bdot(a, b):
    return jnp.dot(a.astype(BF16), b.astype(BF16), preferred_element_type=F32)


def _bdot_nt(a, b):
    return lax.dot_general(a.astype(BF16), b.astype(BF16), (((1,), (1,)), ((), ())),
                           preferred_element_type=F32)


def _bdot_tn(a, b):
    return lax.dot_general(a.astype(BF16), b.astype(BF16), (((0,), (0,)), ((), ())),
                           preferred_element_type=F32)


def _fdot(a, b):
    return jnp.dot(a, b, precision=lax.Precision.HIGHEST, preferred_element_type=F32)


def _silu(x):
    return x * (1.0 / (1.0 + jnp.exp(-x)))


def _sigmoid(x):
    return 1.0 / (1.0 + jnp.exp(-x))


def _softplus(x):
    return jnp.maximum(x, 0.0) + jnp.log(1.0 + jnp.exp(-jnp.abs(x)))


def _rms(x, gain):
    ms = jnp.mean(x * x, axis=-1, keepdims=True)
    return x * lax.rsqrt(ms + EPS) * gain


def _params(*sem):
    return pltpu.CompilerParams(dimension_semantics=sem, vmem_limit_bytes=VMEM_LIMIT_BYTES)


def _const_spec(shape):
    nd = len(shape)
    return pl.BlockSpec(shape, lambda *_: (0,) * nd, pipeline_mode=pl.Buffered(1))


def _ada_kernel(cond_ref, w_ref, b_ref, o_ref):
    a = _silu(cond_ref[...])
    o_ref[...] = jnp.dot(a, w_ref[...], precision=lax.Precision.HIGHEST,
                         preferred_element_type=F32) + b_ref[...]


def _ada(cond8, w_ada, b_ada):
    n_col = 6 * D_MODEL // D_MODEL
    return pl.pallas_call(
        _ada_kernel,
        out_shape=jax.ShapeDtypeStruct((DEPTH, SUBLANES, 6 * D_MODEL), F32),
        grid=(DEPTH, n_col),
        in_specs=[pl.BlockSpec((SUBLANES, D_MODEL), lambda l, j: (0, 0)),
                  pl.BlockSpec((None, D_MODEL, D_MODEL), lambda l, j: (l, 0, j)),
                  pl.BlockSpec((None, 1, D_MODEL), lambda l, j: (l, 0, j))],
        out_specs=pl.BlockSpec((None, SUBLANES, D_MODEL), lambda l, j: (l, 0, j)),
        compiler_params=_params("arbitrary", "arbitrary"),
        name="ada",
    )(cond8, w_ada, b_ada.reshape(DEPTH, 1, 6 * D_MODEL))


def _rope_lanes(x, c, s_up, s_dn):
    half = MLA_ROPE // 2
    return (x * c + pltpu.roll(x, half, 1) * s_up + pltpu.roll(x, LANES - half, 1) * s_dn)


def _inproj_kernel(use_rope, *refs):
    if use_rope:
        (x_ref, mod_ref, gpre_ref, win_ref, qn_ref, wuq_ref, kvn_ref, wk_ref, wv_ref, pk_ref,
         rope_ref, qkv_o, gate_o, qm_o, ckv_o, kmat_o, vmat_o, z_o, xbc_o, small_o) = refs
    else:
        (x_ref, mod_ref, gpre_ref, win_ref, qn_ref, wuq_ref, kvn_ref, wk_ref, wv_ref, pk_ref,
         qkv_o, gate_o, qm_o, ckv_o, kmat_o, vmat_o, z_o, xbc_o, small_o) = refs
    x = x_ref[...]
    sh1 = mod_ref[:, 0:D_MODEL]
    sc1 = mod_ref[:, D_MODEL:2 * D_MODEL]
    h = _rms(x, gpre_ref[...]) * (1.0 + sc1) + sh1
    proj = _bdot(h, win_ref[...])
    qkv_o[...] = proj[:, PC_QKV:PC_GATE]
    gate_o[...] = proj[:, PC_GATE:PC_CQ]
    z_o[...] = proj[:, PC_Z:PC_XBC]
    xbc_o[...] = proj[:, PC_XBC:PC_SMALL]
    small = proj[:, PC_SMALL:PC_TOTAL]
    small_o[...] = small

    cq = _rms(proj[:, PC_CQ:PC_CKV], qn_ref[...])
    qm = _bdot(cq, wuq_ref[...]) * ((MLA_NOPE + MLA_ROPE) ** -0.5 * LOG2_E)
    ckv = _rms(proj[:, PC_CKV:PC_Z], kvn_ref[...])
    ckv_o[...] = ckv
    if use_rope:
        cq_t = rope_ref[:, 0:LANES]
        s1q_t = rope_ref[:, LANES:2 * LANES]
        s2q_t = rope_ref[:, 2 * LANES:3 * LANES]
        for hd in range(MLA_HEADS):
            blk = qm[:, hd * HEAD_BLOCK:(hd + 1) * HEAD_BLOCK]
            qm_o[:, hd * HEAD_BLOCK:(hd + 1) * HEAD_BLOCK] = _rope_lanes(
                blk, cq_t, s1q_t, s2q_t).astype(BF16)
        kr = _rope_lanes(small, rope_ref[:, 3 * LANES:4 * LANES],
                         rope_ref[:, 4 * LANES:5 * LANES], rope_ref[:, 5 * LANES:6 * LANES])
    else:
        qm_o[...] = qm.astype(BF16)
        kr = small
    kmat_o[...] = (_bdot(ckv, wk_ref[...]) + _bdot(kr, pk_ref[...])).astype(BF16)
    vmat_o[...] = _bdot_nt(wv_ref[...], ckv).astype(BF16)


def _inproj(x, mod, per_batch_mod, lw, rope_tab, tm):
    bsz, length, _ = x.shape
    use_rope = rope_tab is not None
    nb = length // tm
    tok = lambda w: pl.BlockSpec((None, tm, w), lambda b, i: (b, i, 0))
    mod_map = (lambda b, i: (b, 0, 0)) if per_batch_mod else (lambda b, i: (0, 0, 0))
    in_specs = [tok(D_MODEL),
                pl.BlockSpec((None, 1, 6 * D_MODEL), mod_map),
                _const_spec((1, D_MODEL)),
                _const_spec((D_MODEL, PC_TOTAL)),
                _const_spec((1, MLA_Q_LORA)),
                _const_spec((MLA_Q_LORA, MLA_HEADS * HEAD_BLOCK)),
                _const_spec((1, MLA_KV_LORA)),
                _const_spec((MLA_KV_LORA, MLA_HEADS * HEAD_BLOCK)),
                _const_spec((MLA_WIDTH, MLA_KV_LORA)),
                _const_spec((LANES, MLA_HEADS * HEAD_BLOCK))]
    args = [x, mod, lw['norm_mix_pre'], lw['w_in'], lw['mla_q_norm'], lw['w_uq'],
            lw['mla_kv_norm'], lw['w_k'], lw['w_vt'], lw['p_k']]
    if use_rope:
        in_specs.append(pl.BlockSpec((tm, 6 * LANES), lambda b, i: (i, 0)))
        args.append(rope_tab)
    widths = [(3 * GDN_WIDTH, F32), (GDN_WIDTH, F32), (MLA_HEADS * HEAD_BLOCK, BF16),
              (MLA_KV_LORA, F32), (MLA_HEADS * HEAD_BLOCK, BF16), None,
              (SSD_INNER, F32), (SSD_CONV_CH, F32), (LANES, F32)]
    out_shape = [jax.ShapeDtypeStruct((bsz, MLA_WIDTH, length), BF16) if w is None
                 else jax.ShapeDtypeStruct((bsz, length, w[0]), w[1]) for w in widths]
    out_specs = [pl.BlockSpec((None, MLA_WIDTH, tm), lambda b, i: (b, 0, i)) if w is None
                 else tok(w[0]) for w in widths]
    return pl.pallas_call(
        functools.partial(_inproj_kernel, use_rope),
        out_shape=out_shape,
        grid=(bsz, nb),
        in_specs=in_specs,
        out_specs=out_specs,
        compiler_params=_params("parallel", "parallel"),
        name="inproj",
    )(*args)


def _ctxkv_kernel(ckv_ref, kr_ref, wk_ref, wv_ref, pk_ref, kmat_o, vmat_o):
    ckv = ckv_ref[...]
    kmat_o[...] = (_bdot(ckv, wk_ref[...]) + _bdot(kr_ref[...], pk_ref[...])).astype(BF16)
    vmat_o[...] = _bdot_nt(wv_ref[...], ckv).astype(BF16)


def _ctxkv(ckv, kr_pad, lw):
    bsz, length, _ = ckv.shape
    tok = lambda w: pl.BlockSpec((None, length, w), lambda b: (b, 0, 0))
    return pl.pallas_call(
        _ctxkv_kernel,
        out_shape=[jax.ShapeDtypeStruct((bsz, length, MLA_HEADS * HEAD_BLOCK), BF16),
                   jax.ShapeDtypeStruct((bsz, MLA_WIDTH, length), BF16)],
        grid=(bsz,),
        in_specs=[tok(MLA_KV_LORA), tok(LANES),
                  _const_spec((MLA_KV_LORA, MLA_HEADS * HEAD_BLOCK)),
                  _const_spec((MLA_WIDTH, MLA_KV_LORA)),
                  _const_spec((LANES, MLA_HEADS * HEAD_BLOCK))],
        out_specs=[tok(MLA_HEADS * HEAD_BLOCK),
                   pl.BlockSpec((None, MLA_WIDTH, length), lambda b: (b, 0, 0))],
        compiler_params=_params("parallel"),
        name="ctxkv",
    )(ckv, kr_pad, lw['w_k'], lw['w_vt'], lw['p_k'])


ATTN_HEADS = 4
ATTN_LOOKAHEAD = 2


def _attn_kernel(n_seg, *refs):
    q_ref = refs[0]
    k_refs = refs[1:1 + n_seg]
    v_refs = refs[1 + n_seg:1 + 2 * n_seg]
    o_ref = refs[1 + 2 * n_seg]
    def scores(hh):
        return [lax.dot_general(k[:, hh * HEAD_BLOCK:(hh + 1) * HEAD_BLOCK],
                                q_ref[:, hh * HEAD_BLOCK:(hh + 1) * HEAD_BLOCK],
                                (((1,), (1,)), ((), ())), preferred_element_type=F32)
                for k in k_refs]

    def softmax(s):
        m = s[0].max(axis=0, keepdims=True)
        for t in s[1:]:
            m = jnp.maximum(m, t.max(axis=0, keepdims=True))
        ps = [jnp.exp2(t - m) for t in s]
        d = ps[0].sum(axis=0, keepdims=True)
        for t in ps[1:]:
            d = d + t.sum(axis=0, keepdims=True)
        return [t.astype(BF16) for t in ps], d

    def values(hh, p, den):
        acc = None
        for t, vt in zip(p, v_refs):
            pv = jnp.dot(vt[hh * MLA_V:(hh + 1) * MLA_V, :], t, preferred_element_type=F32)
            acc = pv if acc is None else acc + pv
        return acc / den

    pending = {hh: scores(hh) for hh in range(min(ATTN_LOOKAHEAD, ATTN_HEADS))}
    outs = []
    for hh in range(ATTN_HEADS):
        p, den = softmax(pending.pop(hh))
        outs.append(values(hh, p, den))
        if hh + ATTN_LOOKAHEAD < ATTN_HEADS:
            pending[hh + ATTN_LOOKAHEAD] = scores(hh + ATTN_LOOKAHEAD)
    o_ref[...] = jnp.concatenate(outs, axis=0).T.astype(BF16)


def _attention(qm, kmats, vmats, tq):
    bsz, length, _ = qm.shape
    n_seg = len(kmats)
    n_grp = MLA_HEADS // ATTN_HEADS
    in_specs = [pl.BlockSpec((None, tq, ATTN_HEADS * HEAD_BLOCK), lambda b, p, i: (b, i, p))]
    for k in kmats:
        in_specs.append(pl.BlockSpec((None, k.shape[1], ATTN_HEADS * HEAD_BLOCK),
                                     lambda b, p, i: (b, 0, p)))
    for v in vmats:
        in_specs.append(pl.BlockSpec((None, ATTN_HEADS * MLA_V, v.shape[2]),
                                     lambda b, p, i: (b, p, 0)))
    return pl.pallas_call(
        functools.partial(_attn_kernel, n_seg),
        out_shape=jax.ShapeDtypeStruct((bsz, length, MLA_WIDTH), BF16),
        grid=(bsz, n_grp, length // tq),
        in_specs=in_specs,
        out_specs=pl.BlockSpec((None, tq, ATTN_HEADS * MLA_V), lambda b, p, i: (b, i, p)),
        compiler_params=_params("parallel", "parallel", "arbitrary"),
        name="attn",
    )(qm, *kmats, *vmats)


def _conv_tile(xe_ref, prev_ref, main_ref, next_ref, w_ref, blk, n_blk, tb):
    xe_ref[0:HALO, :] = jnp.where(blk > 0, prev_ref[...], 0.0)
    xe_ref[HALO:HALO + tb, :] = main_ref[...]
    xe_ref[HALO + tb:HALO + tb + HALO, :] = jnp.where(blk < n_blk - 1, next_ref[...], 0.0)
    xe = xe_ref[...]
    rows = tb + 2 * HALO
    acc = None
    for j in range(CONV_K):
        shift = (CONV_K // 2 - j) % rows
        tap = xe if shift == 0 else pltpu.roll(xe, shift, 0)
        term = tap * w_ref[j:j + 1, :]
        acc = term if acc is None else acc + term
    return acc[HALO:HALO + tb, :]


def _chunk_tri(tb, reverse):
    r = lax.broadcasted_iota(jnp.int32, (tb, tb), 0)
    c = lax.broadcasted_iota(jnp.int32, (tb, tb), 1)
    same = (r // CHUNK) == (c // CHUNK)
    order = (c >= r) if reverse else (c <= r)
    return jnp.where(same & order, 1.0, 0.0).astype(BF16)


def _masks(reverse, width):
    r = lax.broadcasted_iota(jnp.int32, (CHUNK, width), 0)
    c = lax.broadcasted_iota(jnp.int32, (CHUNK, width), 1) % CHUNK
    eye = jnp.where(c == r, 1.0, 0.0).astype(F32)
    if reverse:
        return c >= r, c > r, eye
    return c <= r, c < r, eye


def _lane_block(shape, per):
    return lax.broadcasted_iota(jnp.int32, shape, 1) // per


def _bd(x, n_blk):
    blk = _lane_block(x.shape, x.shape[1] // n_blk)
    return jnp.concatenate([jnp.where(blk == h, x, 0.0) for h in range(n_blk)], axis=0)


def _split2(x):
    hi = x.astype(BF16).astype(F32)
    return hi, x - hi


def _split3(x):
    hi, r1 = _split2(x)
    mid, lo = _split2(r1)
    return hi.astype(BF16), mid.astype(BF16), lo.astype(BF16)


def _sel_dot(x, sel):
    return jnp.dot(jnp.concatenate(_split3(x), axis=1), jnp.concatenate([sel, sel, sel], axis=0),
                   preferred_element_type=F32)


def _tri_dot(tri, x):
    return jnp.dot(jnp.concatenate([tri, tri, tri], axis=1), jnp.concatenate(_split3(x), axis=0),
                   preferred_element_type=F32)


def _lane_select(rows, first, width, per):
    r = lax.broadcasted_iota(jnp.int32, (rows, width), 0)
    c = lax.broadcasted_iota(jnp.int32, (rows, width), 1)
    return jnp.where(r == first + c // per, 1.0, 0.0).astype(BF16)


def _scan_specs(bsz, length, width, tb, reverse):
    n_blk = length // tb
    per = tb // HALO
    n_halo = length // HALO
    blk_of = (lambda i: n_blk - 1 - i) if reverse else (lambda i: i)
    main = pl.BlockSpec((None, tb, width), lambda b, i: (b, blk_of(i), 0))
    prev = pl.BlockSpec((None, HALO, width),
                        lambda b, i: (b, jnp.maximum(blk_of(i) * per - 1, 0), 0))
    nxt = pl.BlockSpec((None, HALO, width),
                       lambda b, i: (b, jnp.minimum((blk_of(i) + 1) * per, n_halo - 1), 0))
    small = pl.BlockSpec((None, tb, LANES), lambda b, i: (b, blk_of(i), 0))
    return n_blk, blk_of, main, prev, nxt, small


def _gdn_kernel(reverse, tb, n_blk, *refs):
    if reverse:
        qkv_s, small_ref, prow_ref, s0_ref, o_ref, sfin_ref, g_s, b_s, state = refs
    else:
        (main_ref, prev_ref, next_ref, small_ref, w_ref, prow_ref, s0_ref,
         o_ref, sfin_ref, qkv_s, xe_ref, g_s, b_s, state) = refs
    i = pl.program_id(1)
    d = 1 if reverse else 0

    @pl.when(i == 0)
    def _():
        state[...] = s0_ref[...]

    nh = GDN_HEADS
    if not reverse:
        qkv = _silu(_conv_tile(xe_ref, prev_ref, main_ref, next_ref, w_ref, i, n_blk, tb))
        q = qkv[:, 0:GDN_WIDTH]
        k = qkv[:, GDN_WIDTH:2 * GDN_WIDTH]
        head_ones = jnp.where(
            lax.broadcasted_iota(jnp.int32, (GDN_WIDTH, GDN_WIDTH), 0) // GDN_DK
            == _lane_block((GDN_WIDTH, GDN_WIDTH), GDN_DK), 1.0, 0.0).astype(BF16)
        qkv_s[:, 0:GDN_WIDTH] = q * lax.rsqrt(_sel_dot(q * q, head_ones) + EPS) * (GDN_DK ** -0.5)
        qkv_s[:, GDN_WIDTH:2 * GDN_WIDTH] = k * lax.rsqrt(_sel_dot(k * k, head_ones) + EPS)
        qkv_s[:, 2 * GDN_WIDTH:3 * GDN_WIDTH] = qkv[:, 2 * GDN_WIDTH:3 * GDN_WIDTH]

    small = small_ref[...]
    log_a = -jnp.exp(prow_ref[0:1, :]) * _softplus(small + prow_ref[1:2, :])
    packed = jnp.where(lax.broadcasted_iota(jnp.int32, small.shape, 1) < SM_GB, log_a,
                       _sigmoid(small))
    sel = jnp.concatenate([_lane_select(LANES, SM_GA + d * nh, GDN_WIDTH, GDN_DK),
                           _lane_select(LANES, SM_GB + d * nh, GDN_WIDTH, GDN_DK)], axis=1)
    spread = _sel_dot(packed, sel)
    g_s[...] = _tri_dot(_chunk_tri(tb, reverse), spread[:, 0:GDN_WIDTH])
    b_s[...] = spread[:, GDN_WIDTH:2 * GDN_WIDTH]

    incl, strict, eye = _masks(reverse, GDN_WIDTH)
    n_chunk = tb // CHUNK
    last = 0 if reverse else CHUNK - 1
    chunks = range(n_chunk)
    row_sl = [slice(c * CHUNK, (c + 1) * CHUNK) for c in chunks]

    g_col = [g_s[r, :] for r in row_sl]
    beta = [b_s[r, :] for r in row_sl]
    g_last = [g[last:last + 1, :] for g in g_col]
    decay = []
    for g in g_col:
        g_row = jnp.sum(eye * g, axis=0, keepdims=True)
        decay.append(jnp.where(incl, jnp.exp(jnp.where(incl, g - g_row, 0.0)), 0.0))
    q4 = [qkv_s[r, 0:GDN_WIDTH] for r in row_sl]
    k4 = [qkv_s[r, GDN_WIDTH:2 * GDN_WIDTH] for r in row_sl]
    qk_kk = [_bdot_nt(jnp.concatenate([q4[c], k4[c]], axis=0), _bd(k4[c], nh)) for c in chunks]
    m_pow = [-jnp.where(strict, qk_kk[c][CHUNK:2 * CHUNK, :] * decay[c], 0.0) * beta[c]
             for c in chunks]
    t_inv = [eye + m for m in m_pow]
    n_sq = 6
    for step in range(n_sq):
        for c in chunks:
            m_hi, m_lo = _split2(m_pow[c])
            rhs = jnp.concatenate(
                [_bd(m_hi, nh).astype(BF16)] * 2 + [_bd(m_lo, nh).astype(BF16)], axis=0)
            if step == 0:
                lhs_f = [(m_hi, m_lo)]
            elif step == n_sq - 1:
                lhs_f = [_split2(t_inv[c])]
            else:
                lhs_f = [(m_hi, m_lo), _split2(t_inv[c])]
            hi = jnp.concatenate([p[0] for p in lhs_f], axis=0).astype(BF16)
            lo = jnp.concatenate([p[1] for p in lhs_f], axis=0).astype(BF16)
            prod = jnp.dot(jnp.concatenate([hi, lo, hi], axis=1), rhs, preferred_element_type=F32)
            if step == 0:
                m_pow[c] = prod
            elif step == n_sq - 1:
                t_inv[c] = t_inv[c] + prod
            else:
                t_inv[c] = t_inv[c] + prod[CHUNK:2 * CHUNK, :]
                m_pow[c] = prod[0:CHUNK, :]
    uw, attn, q_dec, k_dec = [], [], [], []
    for c in chunks:
        eg = jnp.exp(g_col[c])
        v4 = qkv_s[row_sl[c], 2 * GDN_WIDTH:3 * GDN_WIDTH]
        uw.append(_bdot(t_inv[c], jnp.concatenate(
            [_bd(v4 * beta[c], nh), _bd(k4[c] * (beta[c] * eg), nh)], axis=1)))
        attn.append(jnp.where(incl, qk_kk[c][0:CHUNK, :] * decay[c], 0.0))
        q_dec.append(q4[c] * eg)
        k_dec.append(k4[c] * jnp.exp(g_last[c] - g_col[c]))

    blk_id = _lane_block((GDN_DK, GDN_WIDTH), GDN_DV)
    s4 = state[...]
    for ci in chunks:
        c = (n_chunk - 1 - ci) if reverse else ci
        ws_qs = _bdot(jnp.concatenate([uw[c][:, GDN_WIDTH:2 * GDN_WIDTH], q_dec[c]], axis=0),
                      _bd(s4, nh))
        v_new = uw[c][:, 0:GDN_WIDTH] - ws_qs[0:CHUNK, :]
        o_ref[row_sl[c], :] = ws_qs[CHUNK:2 * CHUNK, :] + _bdot(attn[c], _bd(v_new, nh))
        kv = _bdot_tn(k_dec[c], v_new)
        upd = None
        for hd in range(nh):
            part = jnp.where(blk_id == hd, kv[hd * GDN_DK:(hd + 1) * GDN_DK, :], 0.0)
            upd = part if upd is None else upd + part
        s4 = s4 * jnp.exp(g_last[c]) + upd
    state[...] = s4
    sfin_ref[...] = s4


def _gdn(qkv, small, conv_w, prow, s0, reverse, tb):
    bsz, length, width = qkv.shape
    n_blk, blk_of, main, prev, nxt, sm = _scan_specs(bsz, length, width, tb, reverse)
    st = pl.BlockSpec((None, GDN_DK, GDN_WIDTH), lambda b, i: (b, 0, 0))
    s0 = s0.transpose(0, 2, 1, 3).reshape(bsz, GDN_DK, GDN_WIDTH)
    out_shape = [jax.ShapeDtypeStruct((bsz, length, GDN_WIDTH), F32),
                 jax.ShapeDtypeStruct((bsz, GDN_DK, GDN_WIDTH), F32)]
    out_specs = [pl.BlockSpec((None, tb, GDN_WIDTH), lambda b, i: (b, blk_of(i), 0)), st]
    scratch = [pltpu.VMEM((tb, GDN_WIDTH), F32), pltpu.VMEM((tb, GDN_WIDTH), F32),
               pltpu.VMEM((GDN_DK, GDN_WIDTH), F32)]
    if reverse:
        in_specs = [main, sm, _const_spec((SUBLANES, LANES)), st]
        args = (qkv, small, prow, s0)
    else:
        in_specs = [main, prev, nxt, sm, _const_spec((SUBLANES, width)),
                    _const_spec((SUBLANES, LANES)), st]
        args = (qkv, qkv, qkv, small, conv_w, prow, s0)
        out_shape.append(jax.ShapeDtypeStruct((bsz, length, width), F32))
        out_specs.append(main)
        scratch = [pltpu.VMEM((tb + 2 * HALO, width), F32)] + scratch
    res = pl.pallas_call(
        functools.partial(_gdn_kernel, reverse, tb, n_blk),
        out_shape=out_shape,
        grid=(bsz, n_blk),
        in_specs=in_specs,
        out_specs=out_specs,
        scratch_shapes=scratch,
        compiler_params=_params("parallel", "arbitrary"),
        name="gdn_bwd" if reverse else "gdn_fwd",
    )(*args)
    s_fin = res[1].reshape(bsz, GDN_DK, GDN_HEADS, GDN_DV).transpose(0, 2, 1, 3)
    return (res[0], s_fin) + tuple(res[2:])


def _ssd_kernel(reverse, tb, n_blk, *refs):
    if reverse:
        xbc_s, small_ref, prow_ref, s0_ref, y_ref, sfin_ref, dt_s, a_s, state = refs
    else:
        (main_ref, prev_ref, next_ref, small_ref, w_ref, cb_ref, prow_ref, dvec_ref, s0_ref,
         y_ref, sfin_ref, xbc_s, xe_ref, dt_s, a_s, state) = refs
    i = pl.program_id(1)
    d = 1 if reverse else 0

    @pl.when(i == 0)
    def _():
        state[...] = s0_ref[...]

    if not reverse:
        xbc_s[...] = _silu(_conv_tile(xe_ref, prev_ref, main_ref, next_ref, w_ref, i, n_blk, tb)
                           + cb_ref[...])
    b_lo = SSD_INNER
    c_lo = SSD_INNER + SSD_GROUPS * SSD_STATE
    dt = _softplus(small_ref[...] + prow_ref[1:2, :])
    sel = _lane_select(LANES, SM_DT + d * SSD_HEADS, SSD_INNER, SSD_HEAD_DIM)
    spread = _sel_dot(jnp.concatenate([dt * (-jnp.exp(prow_ref[0:1, :])), dt], axis=0), sel)
    a_s[...] = _tri_dot(_chunk_tri(tb, reverse), spread[0:tb, :])
    dt_s[...] = spread[tb:2 * tb, :]

    incl, _, eye = _masks(reverse, SSD_INNER)
    n_chunk = tb // CHUNK
    rep = SSD_HEADS // SSD_GROUPS
    last = 0 if reverse else CHUNK - 1
    grp_of_lane = _lane_block((CHUNK, SSD_GROUPS * SSD_STATE), SSD_STATE)
    grp_of_lane_s = _lane_block((SSD_STATE, SSD_INNER), rep * SSD_HEAD_DIM)
    for ci in range(n_chunk):
        cc = (n_chunk - 1 - ci) if reverse else ci
        rows = slice(cc * CHUNK, (cc + 1) * CHUNK)
        a_col = a_s[rows, :]
        a_row = jnp.sum(eye * a_col, axis=0, keepdims=True)
        a_last = a_col[last:last + 1, :]
        lmat = jnp.where(incl, jnp.exp(jnp.where(incl, a_col - a_row, 0.0)), 0.0)
        xs = xbc_s[rows, 0:b_lo]
        bm = xbc_s[rows, b_lo:c_lo]
        cm = xbc_s[rows, c_lo:SSD_CONV_CH]
        xdt = xs * dt_s[rows, :]
        b_heads = jnp.concatenate(
            [jnp.where(grp_of_lane == hd // rep, bm, 0.0) for hd in range(SSD_HEADS)], axis=0)
        scores = _bdot_nt(cm, b_heads) * lmat
        st = state[...]
        s_groups = jnp.concatenate(
            [jnp.where(grp_of_lane_s == g, st, 0.0) for g in range(SSD_GROUPS)], axis=0)
        y = _bdot(scores, _bd(xdt, SSD_HEADS)) + _bdot(cm, s_groups) * jnp.exp(a_col)
        if not reverse:
            y = y + xs * dvec_ref[...]
        y_ref[rows, :] = y
        cs = _bdot_tn(bm, xdt * jnp.exp(a_last - a_col))
        upd = jnp.where(grp_of_lane_s == 0, cs[0:SSD_STATE, :], cs[SSD_STATE:2 * SSD_STATE, :])
        state[...] = st * jnp.exp(a_last) + upd
    sfin_ref[...] = state[...]


def _ssd(xbc, small, conv_w, conv_b, prow, dvec, s0, reverse, tb):
    bsz, length, width = xbc.shape
    n_blk, blk_of, main, prev, nxt, sm = _scan_specs(bsz, length, width, tb, reverse)
    st = pl.BlockSpec((None, SSD_STATE, SSD_INNER), lambda b, i: (b, 0, 0))
    s0 = s0.transpose(0, 3, 1, 2).reshape(bsz, SSD_STATE, SSD_INNER)
    out_shape = [jax.ShapeDtypeStruct((bsz, length, SSD_INNER), F32),
                 jax.ShapeDtypeStruct((bsz, SSD_STATE, SSD_INNER), F32)]
    out_specs = [pl.BlockSpec((None, tb, SSD_INNER), lambda b, i: (b, blk_of(i), 0)), st]
    scratch = [pltpu.VMEM((tb, SSD_INNER), F32), pltpu.VMEM((tb, SSD_INNER), F32),
               pltpu.VMEM((SSD_STATE, SSD_INNER), F32)]
    if reverse:
        in_specs = [main, sm, _const_spec((SUBLANES, LANES)), st]
        args = (xbc, small, prow, s0)
    else:
        in_specs = [main, prev, nxt, sm, _const_spec((SUBLANES, width)), _const_spec((1, width)),
                    _const_spec((SUBLANES, LANES)), _const_spec((1, SSD_INNER)), st]
        args = (xbc, xbc, xbc, small, conv_w, conv_b, prow, dvec, s0)
        out_shape.append(jax.ShapeDtypeStruct((bsz, length, width), F32))
        out_specs.append(main)
        scratch = [pltpu.VMEM((tb + 2 * HALO, width), F32)] + scratch
    res = pl.pallas_call(
        functools.partial(_ssd_kernel, reverse, tb, n_blk),
        out_shape=out_shape,
        grid=(bsz, n_blk),
        in_specs=in_specs,
        out_specs=out_specs,
        scratch_shapes=scratch,
        compiler_params=_params("parallel", "arbitrary"),
        name="ssd_bwd" if reverse else "ssd_fwd",
    )(*args)
    s_fin = res[1].reshape(bsz, SSD_STATE, SSD_HEADS, SSD_HEAD_DIM).transpose(0, 2, 3, 1)
    return (res[0], s_fin) + tuple(res[2:])


def _outproj_kernel(x_ref, mod_ref, of_ref, ob_ref, gate_ref, om_ref, yf_ref, yb_ref, z_ref,
                    gn_ref, sn_ref, wout_ref, gpost_ref, o_ref):
    og = of_ref[...] + ob_ref[...]
    parts = []
    for hd in range(GDN_HEADS):
        lo = hd * GDN_DV
        oh = og[:, lo:lo + GDN_DV]
        parts.append(oh * lax.rsqrt(jnp.mean(oh * oh, axis=-1, keepdims=True) + EPS))
    o_gdn = jnp.concatenate(parts, axis=-1) * gn_ref[...] * _silu(gate_ref[...])
    o_ssd = _rms((yf_ref[...] + yb_ref[...]) * _silu(z_ref[...]), sn_ref[...])
    mixed = (_bdot(o_gdn, wout_ref[0:GDN_WIDTH, :])
             + jnp.dot(om_ref[...], wout_ref[GDN_WIDTH:GDN_WIDTH + MLA_WIDTH, :],
                       preferred_element_type=F32)
             + _bdot(o_ssd, wout_ref[GDN_WIDTH + MLA_WIDTH:, :]))
    g1 = mod_ref[:, 2 * D_MODEL:3 * D_MODEL]
    o_ref[...] = x_ref[...] + g1 * _rms(mixed, gpost_ref[...])


def _outproj(x, mod, per_batch_mod, o_f, o_b, gate, o_mla, y_f, y_b, z, lw, tm):
    bsz, length, _ = x.shape
    tok = lambda w: pl.BlockSpec((None, tm, w), lambda b, i: (b, i, 0))
    mod_map = (lambda b, i: (b, 0, 0)) if per_batch_mod else (lambda b, i: (0, 0, 0))
    return pl.pallas_call(
        _outproj_kernel,
        out_shape=jax.ShapeDtypeStruct((bsz, length, D_MODEL), F32),
        grid=(bsz, length // tm),
        in_specs=[tok(D_MODEL), pl.BlockSpec((None, 1, 6 * D_MODEL), mod_map),
                  tok(GDN_WIDTH), tok(GDN_WIDTH), tok(GDN_WIDTH), tok(MLA_WIDTH),
                  tok(SSD_INNER), tok(SSD_INNER), tok(SSD_INNER),
                  _const_spec((1, GDN_WIDTH)), _const_spec((1, SSD_INNER)),
                  _const_spec((D_MODEL, D_MODEL)), _const_spec((1, D_MODEL))],
        out_specs=tok(D_MODEL),
        compiler_params=_params("parallel", "parallel"),
        name="outproj",
    )(x, mod, o_f, o_b, gate, o_mla, y_f, y_b, z, lw['gdn_norm'], lw['ssd_norm'], lw['w_out'],
      lw['norm_mix_post'])


FFN_COLS = 256


def _ffn_kernel(tm, n_blk, main_ref, prev_ref, next_ref, mod_ref, gpre_ref, wup_ref, cw_ref,
                wdn_ref, gpost_ref, o_ref, xe_ref):
    i = pl.program_id(1)
    rows = tm + 2 * HALO
    xe_ref[0:HALO, :] = prev_ref[...]
    xe_ref[HALO:HALO + tm, :] = main_ref[...]
    xe_ref[HALO + tm:rows, :] = next_ref[...]
    sh2 = mod_ref[:, 3 * D_MODEL:4 * D_MODEL]
    sc2 = mod_ref[:, 4 * D_MODEL:5 * D_MODEL]
    g2 = mod_ref[:, 5 * D_MODEL:6 * D_MODEL]
    h = _rms(xe_ref[...], gpre_ref[...]) * (1.0 + sc2) + sh2
    r = lax.broadcasted_iota(jnp.int32, (rows, 1), 0)
    valid = ((r >= HALO) | (i > 0)) & ((r < HALO + tm) | (i < n_blk - 1))
    hb = jnp.where(valid, h, 0.0).astype(BF16)

    def conv3(up, col0):
        acc = None
        for j in range(FFN_CONV_K):
            shift = (FFN_CONV_K // 2 - j) % rows
            tap = up if shift == 0 else pltpu.roll(up, shift, 0)
            term = tap * cw_ref[j:j + 1, col0:col0 + FFN_COLS]
            acc = term if acc is None else acc + term
        return acc[HALO:HALO + tm, :]

    def up(cb):
        c0 = cb * FFN_COLS
        return (jnp.dot(hb, wup_ref[:, c0:c0 + FFN_COLS], preferred_element_type=F32),
                jnp.dot(hb, wup_ref[:, D_FF + c0:D_FF + c0 + FFN_COLS],
                        preferred_element_type=F32))

    n_slab = D_FF // FFN_COLS
    acc = None
    ahead = 2
    pending = {cb: up(cb) for cb in range(ahead)}
    for cb in range(n_slab):
        c0 = cb * FFN_COLS
        g_up, u_up = pending.pop(cb)
        if cb + ahead < n_slab:
            pending[cb + ahead] = up(cb + ahead)
        act = _silu(conv3(g_up, c0)) * conv3(u_up, D_FF + c0)
        part = _bdot(act, wdn_ref[c0:c0 + FFN_COLS, :])
        acc = part if acc is None else acc + part
    o_ref[...] = main_ref[...] + g2 * _rms(acc, gpost_ref[...])


def _ffn(x, mod, per_batch_mod, lw, tm):
    bsz, length, _ = x.shape
    n_blk = length // tm
    per = tm // HALO
    n_halo = length // HALO
    mod_map = (lambda b, i: (b, 0, 0)) if per_batch_mod else (lambda b, i: (0, 0, 0))
    return pl.pallas_call(
        functools.partial(_ffn_kernel, tm, n_blk),
        out_shape=jax.ShapeDtypeStruct((bsz, length, D_MODEL), F32),
        grid=(bsz, n_blk),
        in_specs=[pl.BlockSpec((None, tm, D_MODEL), lambda b, i: (b, i, 0)),
                  pl.BlockSpec((None, HALO, D_MODEL),
                               lambda b, i: (b, jnp.maximum(i * per - 1, 0), 0)),
                  pl.BlockSpec((None, HALO, D_MODEL),
                               lambda b, i: (b, jnp.minimum((i + 1) * per, n_halo - 1), 0)),
                  pl.BlockSpec((None, 1, 6 * D_MODEL), mod_map),
                  _const_spec((1, D_MODEL)), _const_spec((D_MODEL, 2 * D_FF)),
                  _const_spec((SUBLANES, 2 * D_FF)), _const_spec((D_FF, D_MODEL)),
                  _const_spec((1, D_MODEL))],
        out_specs=pl.BlockSpec((None, tm, D_MODEL), lambda b, i: (b, i, 0)),
        scratch_shapes=[pltpu.VMEM((tm + 2 * HALO, D_MODEL), F32)],
        compiler_params=_params("parallel", "parallel"),
        name="ffn",
    )(x, x, x, mod, lw['norm_ffn_pre'], lw['ffn_w_up'], lw['ffn_conv'], lw['ffn_w_down'],
      lw['norm_ffn_post'])


def _pad_rows(w, rows):
    return jnp.pad(w, ((0, rows - w.shape[0]), (0, 0)))


def _lane_row(values, offset):
    return jnp.pad(values.reshape(1, -1), ((0, 0), (offset, LANES - offset - values.size)))


def _layer_weights(l, p):
    o = IN_OFFSETS
    w_in = p['w_in'][l]
    col = lambda k: w_in[:, o[k]:o[k] + IN_SIZES[k]]
    small = jnp.concatenate([col(6), col(2), col(3), col(9)], axis=1)
    small = jnp.pad(small, ((0, 0), (0, LANES - small.shape[1])))
    w_in_p = jnp.concatenate([col(0), col(1), col(4), col(5), col(7), col(8), small], axis=1)

    head_pad = HEAD_BLOCK - MLA_NOPE - MLA_ROPE
    w_uq = p['mla_w_uq'][l].reshape(MLA_Q_LORA, MLA_HEADS, MLA_NOPE + MLA_ROPE)
    w_uq = jnp.pad(w_uq, ((0, 0), (0, 0), (0, head_pad))).reshape(MLA_Q_LORA, -1)
    w_ukv = p['mla_w_ukv'][l].reshape(MLA_KV_LORA, MLA_HEADS, MLA_NOPE + MLA_V)
    w_k = jnp.pad(w_ukv[:, :, :MLA_NOPE], ((0, 0), (0, 0), (0, HEAD_BLOCK - MLA_NOPE)))
    w_k = w_k.reshape(MLA_KV_LORA, -1)
    w_v = w_ukv[:, :, MLA_NOPE:].reshape(MLA_KV_LORA, -1)
    place = jnp.pad(jnp.eye(MLA_ROPE, dtype=F32), ((0, LANES - MLA_ROPE), (MLA_NOPE, head_pad)))
    p_k = jnp.tile(place, (1, MLA_HEADS))

    gdn_prow = jnp.concatenate([_lane_row(p['gdn_a_log'][l], SM_GA),
                                _lane_row(p['gdn_dt_bias'][l], SM_GA)], axis=0)
    ssd_prow = jnp.concatenate([_lane_row(p['ssd_a_log'][l], SM_DT),
                                _lane_row(p['ssd_dt_bias'][l], SM_DT)], axis=0)
    row = lambda v: v.reshape(1, -1)
    return {
        'norm_mix_pre': row(p['norm_mix_pre'][l]), 'norm_mix_post': row(p['norm_mix_post'][l]),
        'norm_ffn_pre': row(p['norm_ffn_pre'][l]), 'norm_ffn_post': row(p['norm_ffn_post'][l]),
        'w_in': w_in_p.astype(BF16),
        'mla_q_norm': row(p['mla_q_norm'][l]), 'w_uq': w_uq.astype(BF16),
        'mla_kv_norm': row(p['mla_kv_norm'][l]), 'w_k': w_k.astype(BF16),
        'w_vt': w_v.T.astype(BF16), 'p_k': p_k.astype(BF16),
        'gdn_conv': _pad_rows(p['gdn_conv'][l], SUBLANES), 'gdn_prow': _pad_rows(gdn_prow, SUBLANES),
        'gdn_norm': row(jnp.tile(p['gdn_norm'][l], GDN_HEADS)),
        'ssd_conv': _pad_rows(p['ssd_conv'][l], SUBLANES), 'ssd_conv_b': row(p['ssd_conv_b'][l]),
        'ssd_prow': _pad_rows(ssd_prow, SUBLANES),
        'ssd_d': row(jnp.repeat(p['ssd_d'][l], SSD_HEAD_DIM)),
        'ssd_norm': row(p['ssd_norm'][l]),
        'w_out': p['w_out'][l].astype(BF16),
        'ffn_w_up': p['ffn_w_up'][l].astype(BF16),
        'ffn_conv': _pad_rows(p['ffn_conv'][l], SUBLANES),
        'ffn_w_down': p['ffn_w_down'][l].astype(BF16),
    }


def _rope_table(n_tokens):
    t = jnp.arange(n_tokens, dtype=jnp.int32)
    r = (t // GRID_W).astype(F32)
    col = (t % GRID_W).astype(F32)
    nf = MLA_ROPE // 4
    half = MLA_ROPE // 2
    inv = ROPE_THETA ** (-jnp.arange(nf, dtype=F32) / nf)
    ang = jnp.concatenate([r[:, None] * inv, col[:, None] * inv], axis=-1)
    cos, sin = jnp.cos(ang), jnp.sin(ang)
    zeros = jnp.zeros_like(sin)

    def lanes(first, x1, x2, fill):
        out = jnp.full((n_tokens, LANES), fill, F32)
        out = out.at[:, first:first + half].set(x1)
        return out.at[:, first + half:first + 2 * half].set(x2)

    parts = []
    for first in (MLA_NOPE, SM_KR):
        parts += [lanes(first, cos, cos, 1.0), lanes(first, zeros, sin, 0.0),
                  lanes(first, -sin, zeros, 0.0)]
    return jnp.concatenate(parts, axis=-1)


def _trunk_layer(x, mod, per_batch_mod, lw, rope_tab, ctx, tm, tq):
    bsz, length, _ = x.shape
    tb = min(SCAN_BLOCK, length)
    (qkv_pre, gate, qm, ckv, kmat, vmat, z, xbc_pre, small) = _inproj(
        x, mod, per_batch_mod, lw, rope_tab, tm)
    if ctx is None:
        kmats, vmats = [kmat], [vmat]
        s_gdn = jnp.zeros((2, bsz, GDN_HEADS, GDN_DK, GDN_DV), F32)
        s_ssd = jnp.zeros((2, bsz, SSD_HEADS, SSD_HEAD_DIM, SSD_STATE), F32)
    else:
        ctx_ckv, ctx_kr, st_gdn, st_ssd = ctx
        kr_pad = jnp.pad(ctx_kr, ((0, 0), (0, 0), (SM_KR, LANES - SM_KR - MLA_ROPE)))
        k_ctx, v_ctx = _ctxkv(ctx_ckv, kr_pad, lw)
        kmats, vmats = [kmat, k_ctx], [vmat, v_ctx]
        s_gdn = jnp.moveaxis(st_gdn, 1, 0)
        s_ssd = jnp.moveaxis(st_ssd, 1, 0)
    o_f, sg_f, qkv = _gdn(qkv_pre, small, lw['gdn_conv'], lw['gdn_prow'], s_gdn[0], False, tb)
    o_b, sg_b = _gdn(qkv, small, None, lw['gdn_prow'], s_gdn[1], True, tb)
    y_f, ss_f, xbc = _ssd(xbc_pre, small, lw['ssd_conv'], lw['ssd_conv_b'], lw['ssd_prow'],
                          lw['ssd_d'], s_ssd[0], False, tb)
    y_b, ss_b = _ssd(xbc, small, None, None, lw['ssd_prow'], None, s_ssd[1], True, tb)
    o_mla = _attention(qm, kmats, vmats, tq)
    x = _outproj(x, mod, per_batch_mod, o_f, o_b, gate, o_mla, y_f, y_b, z, lw, tm)
    x = _ffn(x, mod, per_batch_mod, lw, tm)
    ctx_out = (ckv, small[:, :, SM_KR:SM_KR + MLA_ROPE], jnp.stack([sg_f, sg_b], axis=1),
               jnp.stack([ss_f, ss_b], axis=1))
    return x, ctx_out


def kernel(x_prompt, x_sample, cache_mla_ckv, cache_mla_krope, state_gdn, state_ssd, c, c_ctx, w_ada, b_ada, norm_mix_pre, norm_mix_post, norm_ffn_pre, norm_ffn_post, w_in, gdn_conv, gdn_a_log, gdn_dt_bias, gdn_norm, mla_q_norm, mla_w_uq, mla_kv_norm, mla_w_ukv, ssd_conv, ssd_conv_b, ssd_a_log, ssd_dt_bias, ssd_d, ssd_norm, w_out, ffn_w_up, ffn_conv, ffn_w_down):
    p = dict(norm_mix_pre=norm_mix_pre, norm_mix_post=norm_mix_post, norm_ffn_pre=norm_ffn_pre,
             norm_ffn_post=norm_ffn_post, w_in=w_in, gdn_conv=gdn_conv, gdn_a_log=gdn_a_log,
             gdn_dt_bias=gdn_dt_bias, gdn_norm=gdn_norm, mla_q_norm=mla_q_norm,
             mla_w_uq=mla_w_uq, mla_kv_norm=mla_kv_norm, mla_w_ukv=mla_w_ukv, ssd_conv=ssd_conv,
             ssd_conv_b=ssd_conv_b, ssd_a_log=ssd_a_log, ssd_dt_bias=ssd_dt_bias, ssd_d=ssd_d,
             ssd_norm=ssd_norm, w_out=w_out, ffn_w_up=ffn_w_up, ffn_conv=ffn_conv,
             ffn_w_down=ffn_w_down)
    dec_batch = x_sample.shape[0]
    cond8 = jnp.concatenate(
        [c, c_ctx[None, :], jnp.zeros((SUBLANES - dec_batch - 1, D_MODEL), F32)], axis=0)
    mod = _ada(cond8, w_ada, b_ada)
    rope_tab = _rope_table(x_sample.shape[1])

    y_prompt, y_sample = x_prompt, x_sample
    ckv_l, kr_l, sg_l, ss_l = [], [], [], []
    for l in range(DEPTH):
        lw = _layer_weights(l, p)
        mod_s = mod[l, 0:dec_batch].reshape(dec_batch, 1, 6 * D_MODEL)
        mod_p = mod[l, dec_batch:dec_batch + 1].reshape(1, 1, 6 * D_MODEL)
        y_prompt, (ckv, kr, sg, ss) = _trunk_layer(
            y_prompt, mod_p, False, lw, None, None, tm=256, tq=256)
        ckv_l.append(ckv)
        kr_l.append(kr)
        sg_l.append(sg)
        ss_l.append(ss)
        y_sample, _ = _trunk_layer(
            y_sample, mod_s, True, lw, rope_tab,
            (cache_mla_ckv[:, l], cache_mla_krope[:, l], state_gdn[:, l], state_ssd[:, l]),
            tm=512, tq=256)
    return (y_prompt, y_sample, jnp.stack(ckv_l, axis=1), jnp.stack(kr_l, axis=1),
            jnp.stack(sg_l, axis=1), jnp.stack(ss_l, axis=1))
```

```python
import functools

import numpy as np
import jax
import jax.numpy as jnp
from jax import lax
from jax.experimental import pallas as pl
from jax.experimental.pallas import tpu as pltpu

F32 = jnp.float32
BF16 = jnp.bfloat16

D_MODEL = 1024
DEPTH = 2
GRID_W = 64
ROPE_THETA = 10000.0
EPS = 1e-6
LOG2_E = 1.4426950408889634
CHUNK = 64
CONV_K = 5
FFN_CONV_K = 3

GDN_HEADS = 4
GDN_DK = 64
GDN_DV = 64
GDN_WIDTH = GDN_HEADS * GDN_DV

MLA_HEADS = 8
MLA_Q_LORA = 256
MLA_KV_LORA = 128
MLA_NOPE = 64
MLA_ROPE = 32
MLA_V = 64
MLA_WIDTH = MLA_HEADS * MLA_V

SSD_HEADS = 4
SSD_HEAD_DIM = 64
SSD_INNER = SSD_HEADS * SSD_HEAD_DIM
SSD_GROUPS = 2
SSD_STATE = 128
SSD_CONV_CH = SSD_INNER + 2 * SSD_GROUPS * SSD_STATE

D_FF = 128 * ((8 * D_MODEL // 3 + 127) // 128)

IN_SIZES = (3 * GDN_WIDTH, GDN_WIDTH, 2 * GDN_HEADS, 2 * GDN_HEADS,
            MLA_Q_LORA, MLA_KV_LORA, MLA_ROPE,
            SSD_INNER, SSD_CONV_CH, 2 * SSD_HEADS)
IN_OFFSETS = tuple(int(o) for o in np.cumsum((0,) + IN_SIZES)[:-1])

LANES = 128
SUBLANES = 8
VMEM_LIMIT_BYTES = 56 * 1024 * 1024

SM_KR = 0
SM_GA = MLA_ROPE
SM_GB = SM_GA + 2 * GDN_HEADS
SM_DT = SM_GB + 2 * GDN_HEADS
PC_QKV = 0
PC_GATE = PC_QKV + 3 * GDN_WIDTH
PC_CQ = PC_GATE + GDN_WIDTH
PC_CKV = PC_CQ + MLA_Q_LORA
PC_Z = PC_CKV + MLA_KV_LORA
PC_XBC = PC_Z + SSD_INNER
PC_SMALL = PC_XBC + SSD_CONV_CH
PC_TOTAL = PC_SMALL + LANES
HEAD_BLOCK = LANES
SCAN_BLOCK = 256
HALO = SUBLANES


def _bdot(a, b):
    return jnp.dot(a.astype(BF16), b.astype(BF16), preferred_element_type=F32)


def _bdot_nt(a, b):
    return lax.dot_general(a.astype(BF16), b.astype(BF16), (((1,), (1,)), ((), ())),
                           preferred_element_type=F32)


def _bdot_tn(a, b):
    return lax.dot_general(a.astype(BF16), b.astype(BF16), (((0,), (0,)), ((), ())),
                           preferred_element_type=F32)


def _fdot(a, b):
    return jnp.dot(a, b, precision=lax.Precision.HIGHEST, preferred_element_type=F32)


def _silu(x):
    return x * (1.0 / (1.0 + jnp.exp(-x)))


def _sigmoid(x):
    return 1.0 / (1.0 + jnp.exp(-x))


def _softplus(x):
    return jnp.maximum(x, 0.0) + jnp.log(1.0 + jnp.exp(-jnp.abs(x)))


def _rms(x, gain):
    ms = jnp.mean(x * x, axis=-1, keepdims=True)
    return x * lax.rsqrt(ms + EPS) * gain


def _params(*sem):
    return pltpu.CompilerParams(dimension_semantics=sem, vmem_limit_bytes=VMEM_LIMIT_BYTES)


def _const_spec(shape):
    nd = len(shape)
    return pl.BlockSpec(shape, lambda *_: (0,) * nd, pipeline_mode=pl.Buffered(1))


def _ada_kernel(cond_ref, w_ref, b_ref, o_ref):
    a = _silu(cond_ref[...])
    o_ref[...] = jnp.dot(a, w_ref[...], precision=lax.Precision.HIGHEST,
                         preferred_element_type=F32) + b_ref[...]


def _ada(cond8, w_ada, b_ada):
    n_col = 6 * D_MODEL // D_MODEL
    return pl.pallas_call(
        _ada_kernel,
        out_shape=jax.ShapeDtypeStruct((DEPTH, SUBLANES, 6 * D_MODEL), F32),
        grid=(DEPTH, n_col),
        in_specs=[pl.BlockSpec((SUBLANES, D_MODEL), lambda l, j: (0, 0)),
                  pl.BlockSpec((None, D_MODEL, D_MODEL), lambda l, j: (l, 0, j)),
                  pl.BlockSpec((None, 1, D_MODEL), lambda l, j: (l, 0, j))],
        out_specs=pl.BlockSpec((None, SUBLANES, D_MODEL), lambda l, j: (l, 0, j)),
        compiler_params=_params("arbitrary", "arbitrary"),
        name="ada",
    )(cond8, w_ada, b_ada.reshape(DEPTH, 1, 6 * D_MODEL))


def _rope_lanes(x, c, s_up, s_dn):
    half = MLA_ROPE // 2
    return (x * c + pltpu.roll(x, half, 1) * s_up + pltpu.roll(x, LANES - half, 1) * s_dn)


def _inproj_kernel(use_rope, *refs):
    if use_rope:
        (x_ref, mod_ref, gpre_ref, win_ref, qn_ref, wuq_ref, kvn_ref, wk_ref, wv_ref, pk_ref,
         rope_ref, qkv_o, gate_o, qm_o, ckv_o, kmat_o, vmat_o, z_o, xbc_o, small_o) = refs
    else:
        (x_ref, mod_ref, gpre_ref, win_ref, qn_ref, wuq_ref, kvn_ref, wk_ref, wv_ref, pk_ref,
         qkv_o, gate_o, qm_o, ckv_o, kmat_o, vmat_o, z_o, xbc_o, small_o) = refs
    x = x_ref[...]
    sh1 = mod_ref[:, 0:D_MODEL]
    sc1 = mod_ref[:, D_MODEL:2 * D_MODEL]
    h = _rms(x, gpre_ref[...]) * (1.0 + sc1) + sh1
    proj = _bdot(h, win_ref[...])
    qkv_o[...] = proj[:, PC_QKV:PC_GATE]
    gate_o[...] = proj[:, PC_GATE:PC_CQ]
    z_o[...] = proj[:, PC_Z:PC_XBC]
    xbc_o[...] = proj[:, PC_XBC:PC_SMALL]
    small = proj[:, PC_SMALL:PC_TOTAL]
    small_o[...] = small

    cq = _rms(proj[:, PC_CQ:PC_CKV], qn_ref[...])
    qm = _bdot(cq, wuq_ref[...]) * ((MLA_NOPE + MLA_ROPE) ** -0.5 * LOG2_E)
    ckv = _rms(proj[:, PC_CKV:PC_Z], kvn_ref[...])
    ckv_o[...] = ckv
    if use_rope:
        cq_t = rope_ref[:, 0:LANES]
        s1q_t = rope_ref[:, LANES:2 * LANES]
        s2q_t = rope_ref[:, 2 * LANES:3 * LANES]
        for hd in range(MLA_HEADS):
            blk = qm[:, hd * HEAD_BLOCK:(hd + 1) * HEAD_BLOCK]
            qm_o[:, hd * HEAD_BLOCK:(hd + 1) * HEAD_BLOCK] = _rope_lanes(
                blk, cq_t, s1q_t, s2q_t).astype(BF16)
        kr = _rope_lanes(small, rope_ref[:, 3 * LANES:4 * LANES],
                         rope_ref[:, 4 * LANES:5 * LANES], rope_ref[:, 5 * LANES:6 * LANES])
    else:
        qm_o[...] = qm.astype(BF16)
        kr = small
    kmat_o[...] = (_bdot(ckv, wk_ref[...]) + _bdot(kr, pk_ref[...])).astype(BF16)
    vmat_o[...] = _bdot_nt(wv_ref[...], ckv).astype(BF16)


def _inproj(x, mod, per_batch_mod, lw, rope_tab, tm):
    bsz, length, _ = x.shape
    use_rope = rope_tab is not None
    nb = length // tm
    tok = lambda w: pl.BlockSpec((None, tm, w), lambda b, i: (b, i, 0))
    mod_map = (lambda b, i: (b, 0, 0)) if per_batch_mod else (lambda b, i: (0, 0, 0))
    in_specs = [tok(D_MODEL),
                pl.BlockSpec((None, 1, 6 * D_MODEL), mod_map),
                _const_spec((1, D_MODEL)),
                _const_spec((D_MODEL, PC_TOTAL)),
                _const_spec((1, MLA_Q_LORA)),
                _const_spec((MLA_Q_LORA, MLA_HEADS * HEAD_BLOCK)),
                _const_spec((1, MLA_KV_LORA)),
                _const_spec((MLA_KV_LORA, MLA_HEADS * HEAD_BLOCK)),
                _const_spec((MLA_WIDTH, MLA_KV_LORA)),
                _const_spec((LANES, MLA_HEADS * HEAD_BLOCK))]
    args = [x, mod, lw['norm_mix_pre'], lw['w_in'], lw['mla_q_norm'], lw['w_uq'],
            lw['mla_kv_norm'], lw['w_k'], lw['w_vt'], lw['p_k']]
    if use_rope:
        in_specs.append(pl.BlockSpec((tm, 6 * LANES), lambda b, i: (i, 0)))
        args.append(rope_tab)
    widths = [(3 * GDN_WIDTH, F32), (GDN_WIDTH, F32), (MLA_HEADS * HEAD_BLOCK, BF16),
              (MLA_KV_LORA, F32), (MLA_HEADS * HEAD_BLOCK, BF16), None,
              (SSD_INNER, F32), (SSD_CONV_CH, F32), (LANES, F32)]
    out_shape = [jax.ShapeDtypeStruct((bsz, MLA_WIDTH, length), BF16) if w is None
                 else jax.ShapeDtypeStruct((bsz, length, w[0]), w[1]) for w in widths]
    out_specs = [pl.BlockSpec((None, MLA_WIDTH, tm), lambda b, i: (b, 0, i)) if w is None
                 else tok(w[0]) for w in widths]
    return pl.pallas_call(
        functools.partial(_inproj_kernel, use_rope),
        out_shape=out_shape,
        grid=(bsz, nb),
        in_specs=in_specs,
        out_specs=out_specs,
        compiler_params=_params("parallel", "parallel"),
        name="inproj",
    )(*args)


def _ctxkv_kernel(ckv_ref, kr_ref, wk_ref, wv_ref, pk_ref, kmat_o, vmat_o):
    ckv = ckv_ref[...]
    kmat_o[...] = (_bdot(ckv, wk_ref[...]) + _bdot(kr_ref[...], pk_ref[...])).astype(BF16)
    vmat_o[...] = _bdot_nt(wv_ref[...], ckv).astype(BF16)


def _ctxkv(ckv, kr_pad, lw):
    bsz, length, _ = ckv.shape
    tok = lambda w: pl.BlockSpec((None, length, w), lambda b: (b, 0, 0))
    return pl.pallas_call(
        _ctxkv_kernel,
        out_shape=[jax.ShapeDtypeStruct((bsz, length, MLA_HEADS * HEAD_BLOCK), BF16),
                   jax.ShapeDtypeStruct((bsz, MLA_WIDTH, length), BF16)],
        grid=(bsz,),
        in_specs=[tok(MLA_KV_LORA), tok(LANES),
                  _const_spec((MLA_KV_LORA, MLA_HEADS * HEAD_BLOCK)),
                  _const_spec((MLA_WIDTH, MLA_KV_LORA)),
                  _const_spec((LANES, MLA_HEADS * HEAD_BLOCK))],
        out_specs=[tok(MLA_HEADS * HEAD_BLOCK),
                   pl.BlockSpec((None, MLA_WIDTH, length), lambda b: (b, 0, 0))],
        compiler_params=_params("parallel"),
        name="ctxkv",
    )(ckv, kr_pad, lw['w_k'], lw['w_vt'], lw['p_k'])


ATTN_HEADS = 8
ATTN_KEY_CHUNK = 1024


def _attn_kernel(n_seg, *refs):
    q_ref = refs[0]
    k_refs = refs[1:1 + n_seg]
    v_refs = refs[1 + n_seg:1 + 2 * n_seg]
    o_ref = refs[1 + 2 * n_seg]
    chunks = []
    for seg, k in enumerate(k_refs):
        for r0 in range(0, k.shape[0], ATTN_KEY_CHUNK):
            chunks.append((seg, r0, min(ATTN_KEY_CHUNK, k.shape[0] - r0)))

    def scores(hh, chunk):
        seg, r0, n = chunk
        s = lax.dot_general(k_refs[seg][r0:r0 + n, hh * HEAD_BLOCK:(hh + 1) * HEAD_BLOCK],
                            q_ref[:, hh * HEAD_BLOCK:(hh + 1) * HEAD_BLOCK],
                            (((1,), (1,)), ((), ())), preferred_element_type=F32)
        return s, s.max(axis=0, keepdims=True)

    def weigh(hh, chunk, s, m):
        seg, r0, n = chunk
        p = jnp.exp2(s - m)
        pv = jnp.dot(v_refs[seg][hh * MLA_V:(hh + 1) * MLA_V, r0:r0 + n], p.astype(BF16),
                     preferred_element_type=F32)
        return pv, p.sum(axis=0, keepdims=True)

    outs = []
    nxt = [scores(0, c) for c in chunks]
    for hh in range(ATTN_HEADS):
        cur = nxt
        nxt = []
        m = cur[0][1]
        for _, mc in cur[1:]:
            m = jnp.maximum(m, mc)
        acc = None
        den = None
        for ci, c in enumerate(chunks):
            if hh + 1 < ATTN_HEADS:
                nxt.append(scores(hh + 1, c))
            pv, d = weigh(hh, c, cur[ci][0], m)
            acc = pv if acc is None else acc + pv
            den = d if den is None else den + d
        outs.append(acc / den)
    o_ref[...] = jnp.concatenate(outs, axis=0).T.astype(BF16)


def _attention(qm, kmats, vmats, tq):
    bsz, length, _ = qm.shape
    n_seg = len(kmats)
    n_grp = MLA_HEADS // ATTN_HEADS
    in_specs = [pl.BlockSpec((None, tq, ATTN_HEADS * HEAD_BLOCK), lambda b, p, i: (b, i, p))]
    for k in kmats:
        in_specs.append(pl.BlockSpec((None, k.shape[1], ATTN_HEADS * HEAD_BLOCK),
                                     lambda b, p, i: (b, 0, p)))
    for v in vmats:
        in_specs.append(pl.BlockSpec((None, ATTN_HEADS * MLA_V, v.shape[2]),
                                     lambda b, p, i: (b, p, 0)))
    return pl.pallas_call(
        functools.partial(_attn_kernel, n_seg),
        out_shape=jax.ShapeDtypeStruct((bsz, length, MLA_WIDTH), BF16),
        grid=(bsz, n_grp, length // tq),
        in_specs=in_specs,
        out_specs=pl.BlockSpec((None, tq, ATTN_HEADS * MLA_V), lambda b, p, i: (b, i, p)),
        compiler_params=_params("parallel", "parallel", "arbitrary"),
        name="attn",
    )(qm, *kmats, *vmats)


def _conv_tile(xe_ref, prev_ref, main_ref, next_ref, w_ref, blk, n_blk, tb):
    xe_ref[0:HALO, :] = jnp.where(blk > 0, prev_ref[...], 0.0)
    xe_ref[HALO:HALO + tb, :] = main_ref[...]
    xe_ref[HALO + tb:HALO + tb + HALO, :] = jnp.where(blk < n_blk - 1, next_ref[...], 0.0)
    xe = xe_ref[...]
    rows = tb + 2 * HALO
    acc = None
    for j in range(CONV_K):
        shift = (CONV_K // 2 - j) % rows
        tap = xe if shift == 0 else pltpu.roll(xe, shift, 0)
        term = tap * w_ref[j:j + 1, :]
        acc = term if acc is None else acc + term
    return acc[HALO:HALO + tb, :]


def _chunk_tri(tb, reverse):
    r = lax.broadcasted_iota(jnp.int32, (tb, tb), 0)
    c = lax.broadcasted_iota(jnp.int32, (tb, tb), 1)
    same = (r // CHUNK) == (c // CHUNK)
    order = (c >= r) if reverse else (c <= r)
    return jnp.where(same & order, 1.0, 0.0).astype(BF16)


def _masks(reverse, width):
    r = lax.broadcasted_iota(jnp.int32, (CHUNK, width), 0)
    c = lax.broadcasted_iota(jnp.int32, (CHUNK, width), 1) % CHUNK
    eye = jnp.where(c == r, 1.0, 0.0).astype(F32)
    if reverse:
        return c >= r, c > r, eye
    return c <= r, c < r, eye


def _lane_block(shape, per):
    return lax.broadcasted_iota(jnp.int32, shape, 1) // per


def _bd(x, n_blk):
    blk = _lane_block(x.shape, x.shape[1] // n_blk)
    return jnp.concatenate([jnp.where(blk == h, x, 0.0) for h in range(n_blk)], axis=0)


def _split2(x):
    hi = x.astype(BF16).astype(F32)
    return hi, x - hi


def _split3(x):
    hi, r1 = _split2(x)
    mid, lo = _split2(r1)
    return hi.astype(BF16), mid.astype(BF16), lo.astype(BF16)


def _sel_dot(x, sel):
    return jnp.dot(jnp.concatenate(_split3(x), axis=1), jnp.concatenate([sel, sel, sel], axis=0),
                   preferred_element_type=F32)


def _tri_dot(tri, x):
    return jnp.dot(jnp.concatenate([tri, tri, tri], axis=1), jnp.concatenate(_split3(x), axis=0),
                   preferred_element_type=F32)


def _lane_select(rows, first, width, per):
    r = lax.broadcasted_iota(jnp.int32, (rows, width), 0)
    c = lax.broadcasted_iota(jnp.int32, (rows, width), 1)
    return jnp.where(r == first + c // per, 1.0, 0.0).astype(BF16)


def _scan_specs(bsz, length, width, tb, reverse):
    n_blk = length // tb
    per = tb // HALO
    n_halo = length // HALO
    blk_of = (lambda i: n_blk - 1 - i) if reverse else (lambda i: i)
    main = pl.BlockSpec((None, tb, width), lambda b, i: (b, blk_of(i), 0))
    prev = pl.BlockSpec((None, HALO, width),
                        lambda b, i: (b, jnp.maximum(blk_of(i) * per - 1, 0), 0))
    nxt = pl.BlockSpec((None, HALO, width),
                       lambda b, i: (b, jnp.minimum((blk_of(i) + 1) * per, n_halo - 1), 0))
    small = pl.BlockSpec((None, tb, LANES), lambda b, i: (b, blk_of(i), 0))
    return n_blk, blk_of, main, prev, nxt, small


def _gdn_kernel(reverse, tb, n_blk, *refs):
    if reverse:
        qkv_s, small_ref, prow_ref, s0_ref, o_ref, sfin_ref, g_s, b_s, state = refs
    else:
        (main_ref, prev_ref, next_ref, small_ref, w_ref, prow_ref, s0_ref,
         o_ref, sfin_ref, qkv_s, xe_ref, g_s, b_s, state) = refs
    i = pl.program_id(1)
    d = 1 if reverse else 0

    @pl.when(i == 0)
    def _():
        state[...] = s0_ref[...]

    nh = GDN_HEADS
    if not reverse:
        qkv = _silu(_conv_tile(xe_ref, prev_ref, main_ref, next_ref, w_ref, i, n_blk, tb))
        q = qkv[:, 0:GDN_WIDTH]
        k = qkv[:, GDN_WIDTH:2 * GDN_WIDTH]
        head_ones = jnp.where(
            lax.broadcasted_iota(jnp.int32, (GDN_WIDTH, GDN_WIDTH), 0) // GDN_DK
            == _lane_block((GDN_WIDTH, GDN_WIDTH), GDN_DK), 1.0, 0.0).astype(BF16)
        qkv_s[:, 0:GDN_WIDTH] = q * lax.rsqrt(_sel_dot(q * q, head_ones) + EPS) * (GDN_DK ** -0.5)
        qkv_s[:, GDN_WIDTH:2 * GDN_WIDTH] = k * lax.rsqrt(_sel_dot(k * k, head_ones) + EPS)
        qkv_s[:, 2 * GDN_WIDTH:3 * GDN_WIDTH] = qkv[:, 2 * GDN_WIDTH:3 * GDN_WIDTH]

    small = small_ref[...]
    log_a = -jnp.exp(prow_ref[0:1, :]) * _softplus(small + prow_ref[1:2, :])
    packed = jnp.where(lax.broadcasted_iota(jnp.int32, small.shape, 1) < SM_GB, log_a,
                       _sigmoid(small))
    sel = jnp.concatenate([_lane_select(LANES, SM_GA + d * nh, GDN_WIDTH, GDN_DK),
                           _lane_select(LANES, SM_GB + d * nh, GDN_WIDTH, GDN_DK)], axis=1)
    spread = _sel_dot(packed, sel)
    g_s[...] = _tri_dot(_chunk_tri(tb, reverse), spread[:, 0:GDN_WIDTH])
    b_s[...] = spread[:, GDN_WIDTH:2 * GDN_WIDTH]

    incl, strict, eye = _masks(reverse, GDN_WIDTH)
    n_chunk = tb // CHUNK
    last = 0 if reverse else CHUNK - 1
    chunks = range(n_chunk)
    row_sl = [slice(c * CHUNK, (c + 1) * CHUNK) for c in chunks]

    g_col = [g_s[r, :] for r in row_sl]
    beta = [b_s[r, :] for r in row_sl]
    g_last = [g[last:last + 1, :] for g in g_col]
    decay = []
    for g in g_col:
        g_row = jnp.sum(eye * g, axis=0, keepdims=True)
        decay.append(jnp.where(incl, jnp.exp(jnp.where(incl, g - g_row, 0.0)), 0.0))
    q4 = [qkv_s[r, 0:GDN_WIDTH] for r in row_sl]
    k4 = [qkv_s[r, GDN_WIDTH:2 * GDN_WIDTH] for r in row_sl]
    qk_kk = [_bdot_nt(jnp.concatenate([q4[c], k4[c]], axis=0), _bd(k4[c], nh)) for c in chunks]
    m_pow = [-jnp.where(strict, qk_kk[c][CHUNK:2 * CHUNK, :] * decay[c], 0.0) * beta[c]
             for c in chunks]
    t_inv = [eye + m for m in m_pow]
    n_sq = 6
    for step in range(n_sq):
        for c in chunks:
            m_hi, m_lo = _split2(m_pow[c])
            if step == 0:
                lhs_f = [(m_hi, m_lo)]
            elif step == n_sq - 1:
                lhs_f = [_split2(t_inv[c])]
            else:
                lhs_f = [(m_hi, m_lo), _split2(t_inv[c])]
            hi = jnp.concatenate([p[0] for p in lhs_f], axis=0).astype(BF16)
            lo = jnp.concatenate([p[1] for p in lhs_f], axis=0).astype(BF16)
            n_rows = hi.shape[0]
            both = jnp.dot(jnp.concatenate([hi, lo], axis=0), _bd(m_hi, nh).astype(BF16),
                           preferred_element_type=F32)
            prod = (both[0:n_rows, :] + both[n_rows:2 * n_rows, :]
                    + jnp.dot(hi, _bd(m_lo, nh).astype(BF16), preferred_element_type=F32))
            if step == 0:
                m_pow[c] = prod
            elif step == n_sq - 1:
                t_inv[c] = t_inv[c] + prod
            else:
                t_inv[c] = t_inv[c] + prod[CHUNK:2 * CHUNK, :]
                m_pow[c] = prod[0:CHUNK, :]
    uw, attn, q_dec, k_dec = [], [], [], []
    for c in chunks:
        eg = jnp.exp(g_col[c])
        v4 = qkv_s[row_sl[c], 2 * GDN_WIDTH:3 * GDN_WIDTH]
        uw.append(_bdot(t_inv[c], jnp.concatenate(
            [_bd(v4 * beta[c], nh), _bd(k4[c] * (beta[c] * eg), nh)], axis=1)))
        attn.append(jnp.where(incl, qk_kk[c][0:CHUNK, :] * decay[c], 0.0))
        q_dec.append(q4[c] * eg)
        k_dec.append(k4[c] * jnp.exp(g_last[c] - g_col[c]))

    blk_id = _lane_block((GDN_DK, GDN_WIDTH), GDN_DV)

    def head_blocks(x):
        out = None
        for hd in range(nh):
            part = jnp.where(blk_id == hd, x[hd * GDN_DK:(hd + 1) * GDN_DK, :], 0.0)
            out = part if out is None else out + part
        return out

    k_u, k_w = [], []
    for c in chunks:
        kuw = _bdot_tn(k_dec[c], uw[c])
        k_u.append(head_blocks(kuw[:, 0:GDN_WIDTH]))
        k_w.append(head_blocks(kuw[:, GDN_WIDTH:2 * GDN_WIDTH]))

    def emit_output(c, s_in):
        ws_qs = _bdot(jnp.concatenate([uw[c][:, GDN_WIDTH:2 * GDN_WIDTH], q_dec[c]], axis=0),
                      _bd(s_in, nh))
        v_new = uw[c][:, 0:GDN_WIDTH] - ws_qs[0:CHUNK, :]
        o_ref[row_sl[c], :] = ws_qs[CHUNK:2 * CHUNK, :] + _bdot(attn[c], _bd(v_new, nh))

    s4 = state[...]
    prev = None
    for ci in chunks:
        c = (n_chunk - 1 - ci) if reverse else ci
        s_in = s4
        s4 = s4 * jnp.exp(g_last[c]) + k_u[c] - _bdot(k_w[c], _bd(s4, nh))
        if prev is not None:
            emit_output(*prev)
        prev = (c, s_in)
    emit_output(*prev)
    state[...] = s4
    sfin_ref[...] = s4


def _gdn(qkv, small, conv_w, prow, s0, reverse, tb):
    bsz, length, width = qkv.shape
    n_blk, blk_of, main, prev, nxt, sm = _scan_specs(bsz, length, width, tb, reverse)
    st = pl.BlockSpec((None, GDN_DK, GDN_WIDTH), lambda b, i: (b, 0, 0))
    s0 = s0.transpose(0, 2, 1, 3).reshape(bsz, GDN_DK, GDN_WIDTH)
    out_shape = [jax.ShapeDtypeStruct((bsz, length, GDN_WIDTH), F32),
                 jax.ShapeDtypeStruct((bsz, GDN_DK, GDN_WIDTH), F32)]
    out_specs = [pl.BlockSpec((None, tb, GDN_WIDTH), lambda b, i: (b, blk_of(i), 0)), st]
    scratch = [pltpu.VMEM((tb, GDN_WIDTH), F32), pltpu.VMEM((tb, GDN_WIDTH), F32),
               pltpu.VMEM((GDN_DK, GDN_WIDTH), F32)]
    if reverse:
        in_specs = [main, sm, _const_spec((SUBLANES, LANES)), st]
        args = (qkv, small, prow, s0)
    else:
        in_specs = [main, prev, nxt, sm, _const_spec((SUBLANES, width)),
                    _const_spec((SUBLANES, LANES)), st]
        args = (qkv, qkv, qkv, small, conv_w, prow, s0)
        out_shape.append(jax.ShapeDtypeStruct((bsz, length, width), F32))
        out_specs.append(main)
        scratch = [pltpu.VMEM((tb + 2 * HALO, width), F32)] + scratch
    res = pl.pallas_call(
        functools.partial(_gdn_kernel, reverse, tb, n_blk),
        out_shape=out_shape,
        grid=(bsz, n_blk),
        in_specs=in_specs,
        out_specs=out_specs,
        scratch_shapes=scratch,
        compiler_params=_params("parallel", "arbitrary"),
        name="gdn_bwd" if reverse else "gdn_fwd",
    )(*args)
    s_fin = res[1].reshape(bsz, GDN_DK, GDN_HEADS, GDN_DV).transpose(0, 2, 1, 3)
    return (res[0], s_fin) + tuple(res[2:])


def _ssd_kernel(reverse, tb, n_blk, *refs):
    if reverse:
        xbc_s, small_ref, prow_ref, s0_ref, y_ref, sfin_ref, dt_s, a_s, state = refs
    else:
        (main_ref, prev_ref, next_ref, small_ref, w_ref, cb_ref, prow_ref, dvec_ref, s0_ref,
         y_ref, sfin_ref, xbc_s, xe_ref, dt_s, a_s, state) = refs
    i = pl.program_id(1)
    d = 1 if reverse else 0

    @pl.when(i == 0)
    def _():
        state[...] = s0_ref[...]

    if not reverse:
        xbc_s[...] = _silu(_conv_tile(xe_ref, prev_ref, main_ref, next_ref, w_ref, i, n_blk, tb)
                           + cb_ref[...])
    b_lo = SSD_INNER
    c_lo = SSD_INNER + SSD_GROUPS * SSD_STATE
    dt = _softplus(small_ref[...] + prow_ref[1:2, :])
    sel = _lane_select(LANES, SM_DT + d * SSD_HEADS, SSD_INNER, SSD_HEAD_DIM)
    spread = _sel_dot(jnp.concatenate([dt * (-jnp.exp(prow_ref[0:1, :])), dt], axis=0), sel)
    a_s[...] = _tri_dot(_chunk_tri(tb, reverse), spread[0:tb, :])
    dt_s[...] = spread[tb:2 * tb, :]

    incl, _, eye = _masks(reverse, SSD_INNER)
    n_chunk = tb // CHUNK
    rep = SSD_HEADS // SSD_GROUPS
    last = 0 if reverse else CHUNK - 1
    grp_of_lane = _lane_block((CHUNK, SSD_GROUPS * SSD_STATE), SSD_STATE)
    grp_of_lane_s = _lane_block((SSD_STATE, SSD_INNER), rep * SSD_HEAD_DIM)
    for ci in range(n_chunk):
        cc = (n_chunk - 1 - ci) if reverse else ci
        rows = slice(cc * CHUNK, (cc + 1) * CHUNK)
        a_col = a_s[rows, :]
        a_row = jnp.sum(eye * a_col, axis=0, keepdims=True)
        a_last = a_col[last:last + 1, :]
        lmat = jnp.where(incl, jnp.exp(jnp.where(incl, a_col - a_row, 0.0)), 0.0)
        xs = xbc_s[rows, 0:b_lo]
        bm = xbc_s[rows, b_lo:c_lo]
        cm = xbc_s[rows, c_lo:SSD_CONV_CH]
        xdt = xs * dt_s[rows, :]
        b_heads = jnp.concatenate(
            [jnp.where(grp_of_lane == hd // rep, bm, 0.0) for hd in range(SSD_HEADS)], axis=0)
        scores = _bdot_nt(cm, b_heads) * lmat
        st = state[...]
        s_groups = jnp.concatenate(
            [jnp.where(grp_of_lane_s == g, st, 0.0) for g in range(SSD_GROUPS)], axis=0)
        y = _bdot(scores, _bd(xdt, SSD_HEADS)) + _bdot(cm, s_groups) * jnp.exp(a_col)
        if not reverse:
            y = y + xs * dvec_ref[...]
        y_ref[rows, :] = y
        cs = _bdot_tn(bm, xdt * jnp.exp(a_last - a_col))
        upd = jnp.where(grp_of_lane_s == 0, cs[0:SSD_STATE, :], cs[SSD_STATE:2 * SSD_STATE, :])
        state[...] = st * jnp.exp(a_last) + upd
    sfin_ref[...] = state[...]


def _ssd(xbc, small, conv_w, conv_b, prow, dvec, s0, reverse, tb):
    bsz, length, width = xbc.shape
    n_blk, blk_of, main, prev, nxt, sm = _scan_specs(bsz, length, width, tb, reverse)
    st = pl.BlockSpec((None, SSD_STATE, SSD_INNER), lambda b, i: (b, 0, 0))
    s0 = s0.transpose(0, 3, 1, 2).reshape(bsz, SSD_STATE, SSD_INNER)
    out_shape = [jax.ShapeDtypeStruct((bsz, length, SSD_INNER), F32),
                 jax.ShapeDtypeStruct((bsz, SSD_STATE, SSD_INNER), F32)]
    out_specs = [pl.BlockSpec((None, tb, SSD_INNER), lambda b, i: (b, blk_of(i), 0)), st]
    scratch = [pltpu.VMEM((tb, SSD_INNER), F32), pltpu.VMEM((tb, SSD_INNER), F32),
               pltpu.VMEM((SSD_STATE, SSD_INNER), F32)]
    if reverse:
        in_specs = [main, sm, _const_spec((SUBLANES, LANES)), st]
        args = (xbc, small, prow, s0)
    else:
        in_specs = [main, prev, nxt, sm, _const_spec((SUBLANES, width)), _const_spec((1, width)),
                    _const_spec((SUBLANES, LANES)), _const_spec((1, SSD_INNER)), st]
        args = (xbc, xbc, xbc, small, conv_w, conv_b, prow, dvec, s0)
        out_shape.append(jax.ShapeDtypeStruct((bsz, length, width), F32))
        out_specs.append(main)
        scratch = [pltpu.VMEM((tb + 2 * HALO, width), F32)] + scratch
    res = pl.pallas_call(
        functools.partial(_ssd_kernel, reverse, tb, n_blk),
        out_shape=out_shape,
        grid=(bsz, n_blk),
        in_specs=in_specs,
        out_specs=out_specs,
        scratch_shapes=scratch,
        compiler_params=_params("parallel", "arbitrary"),
        name="ssd_bwd" if reverse else "ssd_fwd",
    )(*args)
    s_fin = res[1].reshape(bsz, SSD_STATE, SSD_HEADS, SSD_HEAD_DIM).transpose(0, 2, 3, 1)
    return (res[0], s_fin) + tuple(res[2:])


def _outproj_kernel(x_ref, mod_ref, of_ref, ob_ref, gate_ref, om_ref, yf_ref, yb_ref, z_ref,
                    gn_ref, sn_ref, wout_ref, gpost_ref, o_ref):
    og = of_ref[...] + ob_ref[...]
    parts = []
    for hd in range(GDN_HEADS):
        lo = hd * GDN_DV
        oh = og[:, lo:lo + GDN_DV]
        parts.append(oh * lax.rsqrt(jnp.mean(oh * oh, axis=-1, keepdims=True) + EPS))
    o_gdn = jnp.concatenate(parts, axis=-1) * gn_ref[...] * _silu(gate_ref[...])
    o_ssd = _rms((yf_ref[...] + yb_ref[...]) * _silu(z_ref[...]), sn_ref[...])
    mixed = (_bdot(o_gdn, wout_ref[0:GDN_WIDTH, :])
             + jnp.dot(om_ref[...], wout_ref[GDN_WIDTH:GDN_WIDTH + MLA_WIDTH, :],
                       preferred_element_type=F32)
             + _bdot(o_ssd, wout_ref[GDN_WIDTH + MLA_WIDTH:, :]))
    g1 = mod_ref[:, 2 * D_MODEL:3 * D_MODEL]
    o_ref[...] = x_ref[...] + g1 * _rms(mixed, gpost_ref[...])


def _outproj(x, mod, per_batch_mod, o_f, o_b, gate, o_mla, y_f, y_b, z, lw, tm):
    bsz, length, _ = x.shape
    tok = lambda w: pl.BlockSpec((None, tm, w), lambda b, i: (b, i, 0))
    mod_map = (lambda b, i: (b, 0, 0)) if per_batch_mod else (lambda b, i: (0, 0, 0))
    return pl.pallas_call(
        _outproj_kernel,
        out_shape=jax.ShapeDtypeStruct((bsz, length, D_MODEL), F32),
        grid=(bsz, length // tm),
        in_specs=[tok(D_MODEL), pl.BlockSpec((None, 1, 6 * D_MODEL), mod_map),
                  tok(GDN_WIDTH), tok(GDN_WIDTH), tok(GDN_WIDTH), tok(MLA_WIDTH),
                  tok(SSD_INNER), tok(SSD_INNER), tok(SSD_INNER),
                  _const_spec((1, GDN_WIDTH)), _const_spec((1, SSD_INNER)),
                  _const_spec((D_MODEL, D_MODEL)), _const_spec((1, D_MODEL))],
        out_specs=tok(D_MODEL),
        compiler_params=_params("parallel", "parallel"),
        name="outproj",
    )(x, mod, o_f, o_b, gate, o_mla, y_f, y_b, z, lw['gdn_norm'], lw['ssd_norm'], lw['w_out'],
      lw['norm_mix_post'])


FFN_COLS = 256


def _ffn_kernel(tm, n_blk, main_ref, prev_ref, next_ref, mod_ref, gpre_ref, wup_ref, cw_ref,
                wdn_ref, gpost_ref, o_ref, xe_ref):
    i = pl.program_id(1)
    rows = tm + 2 * HALO
    xe_ref[0:HALO, :] = prev_ref[...]
    xe_ref[HALO:HALO + tm, :] = main_ref[...]
    xe_ref[HALO + tm:rows, :] = next_ref[...]
    sh2 = mod_ref[:, 3 * D_MODEL:4 * D_MODEL]
    sc2 = mod_ref[:, 4 * D_MODEL:5 * D_MODEL]
    g2 = mod_ref[:, 5 * D_MODEL:6 * D_MODEL]
    h = _rms(xe_ref[...], gpre_ref[...]) * (1.0 + sc2) + sh2
    r = lax.broadcasted_iota(jnp.int32, (rows, 1), 0)
    valid = ((r >= HALO) | (i > 0)) & ((r < HALO + tm) | (i < n_blk - 1))
    hb = jnp.where(valid, h, 0.0).astype(BF16)

    def conv3(up, col0):
        acc = None
        for j in range(FFN_CONV_K):
            shift = (FFN_CONV_K // 2 - j) % rows
            tap = up if shift == 0 else pltpu.roll(up, shift, 0)
            term = tap * cw_ref[j:j + 1, col0:col0 + FFN_COLS]
            acc = term if acc is None else acc + term
        return acc[HALO:HALO + tm, :]

    def up(cb):
        c0 = cb * FFN_COLS
        return (jnp.dot(hb, wup_ref[:, c0:c0 + FFN_COLS], preferred_element_type=F32),
                jnp.dot(hb, wup_ref[:, D_FF + c0:D_FF + c0 + FFN_COLS],
                        preferred_element_type=F32))

    n_slab = D_FF // FFN_COLS
    acc = None
    ahead = 2
    pending = {cb: up(cb) for cb in range(ahead)}
    for cb in range(n_slab):
        c0 = cb * FFN_COLS
        g_up, u_up = pending.pop(cb)
        if cb + ahead < n_slab:
            pending[cb + ahead] = up(cb + ahead)
        act = _silu(conv3(g_up, c0)) * conv3(u_up, D_FF + c0)
        part = _bdot(act, wdn_ref[c0:c0 + FFN_COLS, :])
        acc = part if acc is None else acc + part
    o_ref[...] = main_ref[...] + g2 * _rms(acc, gpost_ref[...])


def _ffn(x, mod, per_batch_mod, lw, tm):
    bsz, length, _ = x.shape
    n_blk = length // tm
    per = tm // HALO
    n_halo = length // HALO
    mod_map = (lambda b, i: (b, 0, 0)) if per_batch_mod else (lambda b, i: (0, 0, 0))
    return pl.pallas_call(
        functools.partial(_ffn_kernel, tm, n_blk),
        out_shape=jax.ShapeDtypeStruct((bsz, length, D_MODEL), F32),
        grid=(bsz, n_blk),
        in_specs=[pl.BlockSpec((None, tm, D_MODEL), lambda b, i: (b, i, 0)),
                  pl.BlockSpec((None, HALO, D_MODEL),
                               lambda b, i: (b, jnp.maximum(i * per - 1, 0), 0)),
                  pl.BlockSpec((None, HALO, D_MODEL),
                               lambda b, i: (b, jnp.minimum((i + 1) * per, n_halo - 1), 0)),
                  pl.BlockSpec((None, 1, 6 * D_MODEL), mod_map),
                  _const_spec((1, D_MODEL)), _const_spec((D_MODEL, 2 * D_FF)),
                  _const_spec((SUBLANES, 2 * D_FF)), _const_spec((D_FF, D_MODEL)),
                  _const_spec((1, D_MODEL))],
        out_specs=pl.BlockSpec((None, tm, D_MODEL), lambda b, i: (b, i, 0)),
        scratch_shapes=[pltpu.VMEM((tm + 2 * HALO, D_MODEL), F32)],
        compiler_params=_params("parallel", "parallel"),
        name="ffn",
    )(x, x, x, mod, lw['norm_ffn_pre'], lw['ffn_w_up'], lw['ffn_conv'], lw['ffn_w_down'],
      lw['norm_ffn_post'])


def _pad_rows(w, rows):
    return jnp.pad(w, ((0, rows - w.shape[0]), (0, 0)))


def _lane_row(values, offset):
    return jnp.pad(values.reshape(1, -1), ((0, 0), (offset, LANES - offset - values.size)))


def _layer_weights(l, p):
    o = IN_OFFSETS
    w_in = p['w_in'][l]
    col = lambda k: w_in[:, o[k]:o[k] + IN_SIZES[k]]
    small = jnp.concatenate([col(6), col(2), col(3), col(9)], axis=1)
    small = jnp.pad(small, ((0, 0), (0, LANES - small.shape[1])))
    w_in_p = jnp.concatenate([col(0), col(1), col(4), col(5), col(7), col(8), small], axis=1)

    head_pad = HEAD_BLOCK - MLA_NOPE - MLA_ROPE
    w_uq = p['mla_w_uq'][l].reshape(MLA_Q_LORA, MLA_HEADS, MLA_NOPE + MLA_ROPE)
    w_uq = jnp.pad(w_uq, ((0, 0), (0, 0), (0, head_pad))).reshape(MLA_Q_LORA, -1)
    w_ukv = p['mla_w_ukv'][l].reshape(MLA_KV_LORA, MLA_HEADS, MLA_NOPE + MLA_V)
    w_k = jnp.pad(w_ukv[:, :, :MLA_NOPE], ((0, 0), (0, 0), (0, HEAD_BLOCK - MLA_NOPE)))
    w_k = w_k.reshape(MLA_KV_LORA, -1)
    w_v = w_ukv[:, :, MLA_NOPE:].reshape(MLA_KV_LORA, -1)
    place = jnp.pad(jnp.eye(MLA_ROPE, dtype=F32), ((0, LANES - MLA_ROPE), (MLA_NOPE, head_pad)))
    p_k = jnp.tile(place, (1, MLA_HEADS))

    gdn_prow = jnp.concatenate([_lane_row(p['gdn_a_log'][l], SM_GA),
                                _lane_row(p['gdn_dt_bias'][l], SM_GA)], axis=0)
    ssd_prow = jnp.concatenate([_lane_row(p['ssd_a_log'][l], SM_DT),
                                _lane_row(p['ssd_dt_bias'][l], SM_DT)], axis=0)
    row = lambda v: v.reshape(1, -1)
    return {
        'norm_mix_pre': row(p['norm_mix_pre'][l]), 'norm_mix_post': row(p['norm_mix_post'][l]),
        'norm_ffn_pre': row(p['norm_ffn_pre'][l]), 'norm_ffn_post': row(p['norm_ffn_post'][l]),
        'w_in': w_in_p.astype(BF16),
        'mla_q_norm': row(p['mla_q_norm'][l]), 'w_uq': w_uq.astype(BF16),
        'mla_kv_norm': row(p['mla_kv_norm'][l]), 'w_k': w_k.astype(BF16),
        'w_vt': w_v.T.astype(BF16), 'p_k': p_k.astype(BF16),
        'gdn_conv': _pad_rows(p['gdn_conv'][l], SUBLANES), 'gdn_prow': _pad_rows(gdn_prow, SUBLANES),
        'gdn_norm': row(jnp.tile(p['gdn_norm'][l], GDN_HEADS)),
        'ssd_conv': _pad_rows(p['ssd_conv'][l], SUBLANES), 'ssd_conv_b': row(p['ssd_conv_b'][l]),
        'ssd_prow': _pad_rows(ssd_prow, SUBLANES),
        'ssd_d': row(jnp.repeat(p['ssd_d'][l], SSD_HEAD_DIM)),
        'ssd_norm': row(p['ssd_norm'][l]),
        'w_out': p['w_out'][l].astype(BF16),
        'ffn_w_up': p['ffn_w_up'][l].astype(BF16),
        'ffn_conv': _pad_rows(p['ffn_conv'][l], SUBLANES),
        'ffn_w_down': p['ffn_w_down'][l].astype(BF16),
    }


def _rope_table(n_tokens):
    t = jnp.arange(n_tokens, dtype=jnp.int32)
    r = (t // GRID_W).astype(F32)
    col = (t % GRID_W).astype(F32)
    nf = MLA_ROPE // 4
    half = MLA_ROPE // 2
    inv = ROPE_THETA ** (-jnp.arange(nf, dtype=F32) / nf)
    ang = jnp.concatenate([r[:, None] * inv, col[:, None] * inv], axis=-1)
    cos, sin = jnp.cos(ang), jnp.sin(ang)
    zeros = jnp.zeros_like(sin)

    def lanes(first, x1, x2, fill):
        out = jnp.full((n_tokens, LANES), fill, F32)
        out = out.at[:, first:first + half].set(x1)
        return out.at[:, first + half:first + 2 * half].set(x2)

    parts = []
    for first in (MLA_NOPE, SM_KR):
        parts += [lanes(first, cos, cos, 1.0), lanes(first, zeros, sin, 0.0),
                  lanes(first, -sin, zeros, 0.0)]
    return jnp.concatenate(parts, axis=-1)


def _trunk_layer(x, mod, per_batch_mod, lw, rope_tab, ctx, tm, tq):
    bsz, length, _ = x.shape
    tb = min(SCAN_BLOCK, length)
    (qkv_pre, gate, qm, ckv, kmat, vmat, z, xbc_pre, small) = _inproj(
        x, mod, per_batch_mod, lw, rope_tab, tm)
    if ctx is None:
        kmats, vmats = [kmat], [vmat]
        s_gdn = jnp.zeros((2, bsz, GDN_HEADS, GDN_DK, GDN_DV), F32)
        s_ssd = jnp.zeros((2, bsz, SSD_HEADS, SSD_HEAD_DIM, SSD_STATE), F32)
    else:
        ctx_ckv, ctx_kr, st_gdn, st_ssd = ctx
        kr_pad = jnp.pad(ctx_kr, ((0, 0), (0, 0), (SM_KR, LANES - SM_KR - MLA_ROPE)))
        k_ctx, v_ctx = _ctxkv(ctx_ckv, kr_pad, lw)
        kmats, vmats = [kmat, k_ctx], [vmat, v_ctx]
        s_gdn = jnp.moveaxis(st_gdn, 1, 0)
        s_ssd = jnp.moveaxis(st_ssd, 1, 0)
    o_f, sg_f, qkv = _gdn(qkv_pre, small, lw['gdn_conv'], lw['gdn_prow'], s_gdn[0], False, tb)
    o_b, sg_b = _gdn(qkv, small, None, lw['gdn_prow'], s_gdn[1], True, tb)
    y_f, ss_f, xbc = _ssd(xbc_pre, small, lw['ssd_conv'], lw['ssd_conv_b'], lw['ssd_prow'],
                          lw['ssd_d'], s_ssd[0], False, tb)
    y_b, ss_b = _ssd(xbc, small, None, None, lw['ssd_prow'], None, s_ssd[1], True, tb)
    o_mla = _attention(qm, kmats, vmats, tq)
    x = _outproj(x, mod, per_batch_mod, o_f, o_b, gate, o_mla, y_f, y_b, z, lw, tm)
    x = _ffn(x, mod, per_batch_mod, lw, tm)
    ctx_out = (ckv, small[:, :, SM_KR:SM_KR + MLA_ROPE], jnp.stack([sg_f, sg_b], axis=1),
               jnp.stack([ss_f, ss_b], axis=1))
    return x, ctx_out


def kernel(x_prompt, x_sample, cache_mla_ckv, cache_mla_krope, state_gdn, state_ssd, c, c_ctx, w_ada, b_ada, norm_mix_pre, norm_mix_post, norm_ffn_pre, norm_ffn_post, w_in, gdn_conv, gdn_a_log, gdn_dt_bias, gdn_norm, mla_q_norm, mla_w_uq, mla_kv_norm, mla_w_ukv, ssd_conv, ssd_conv_b, ssd_a_log, ssd_dt_bias, ssd_d, ssd_norm, w_out, ffn_w_up, ffn_conv, ffn_w_down):
    p = dict(norm_mix_pre=norm_mix_pre, norm_mix_post=norm_mix_post, norm_ffn_pre=norm_ffn_pre,
             norm_ffn_post=norm_ffn_post, w_in=w_in, gdn_conv=gdn_conv, gdn_a_log=gdn_a_log,
             gdn_dt_bias=gdn_dt_bias, gdn_norm=gdn_norm, mla_q_norm=mla_q_norm,
             mla_w_uq=mla_w_uq, mla_kv_norm=mla_kv_norm, mla_w_ukv=mla_w_ukv, ssd_conv=ssd_conv,
             ssd_conv_b=ssd_conv_b, ssd_a_log=ssd_a_log, ssd_dt_bias=ssd_dt_bias, ssd_d=ssd_d,
             ssd_norm=ssd_norm, w_out=w_out, ffn_w_up=ffn_w_up, ffn_conv=ffn_conv,
             ffn_w_down=ffn_w_down)
    dec_batch = x_sample.shape[0]
    cond8 = jnp.concatenate(
        [c, c_ctx[None, :], jnp.zeros((SUBLANES - dec_batch - 1, D_MODEL), F32)], axis=0)
    mod = _ada(cond8, w_ada, b_ada)
    rope_tab = _rope_table(x_sample.shape[1])

    y_prompt, y_sample = x_prompt, x_sample
    ckv_l, kr_l, sg_l, ss_l = [], [], [], []
    for l in range(DEPTH):
        lw = _layer_weights(l, p)
        mod_s = mod[l, 0:dec_batch].reshape(dec_batch, 1, 6 * D_MODEL)
        mod_p = mod[l, dec_batch:dec_batch + 1].reshape(1, 1, 6 * D_MODEL)
        y_prompt, (ckv, kr, sg, ss) = _trunk_layer(
            y_prompt, mod_p, False, lw, None, None, tm=256, tq=256)
        ckv_l.append(ckv)
        kr_l.append(kr)
        sg_l.append(sg)
        ss_l.append(ss)
        y_sample, _ = _trunk_layer(
            y_sample, mod_s, True, lw, rope_tab,
            (cache_mla_ckv[:, l], cache_mla_krope[:, l], state_gdn[:, l], state_ssd[:, l]),
            tm=512, tq=256)
    return (y_prompt, y_sample, jnp.stack(ckv_l, axis=1), jnp.stack(kr_l, axis=1),
            jnp.stack(sg_l, axis=1), jnp.stack(ss_l, axis=1))
```

```python
import functools
import itertools

import numpy as np
import jax
import jax.numpy as jnp
from jax import lax
from jax.experimental import pallas as pl
from jax.experimental.pallas import tpu as pltpu

F32 = jnp.float32
BF16 = jnp.bfloat16

D_MODEL = 1024
DEPTH = 2
GRID_W = 64
ROPE_THETA = 10000.0
EPS = 1e-6
LOG2_E = 1.4426950408889634
CHUNK = 64
CONV_K = 5
FFN_CONV_K = 3

GDN_HEADS = 4
GDN_DK = 64
GDN_DV = 64
GDN_WIDTH = GDN_HEADS * GDN_DV

MLA_HEADS = 8
MLA_Q_LORA = 256
MLA_KV_LORA = 128
MLA_NOPE = 64
MLA_ROPE = 32
MLA_V = 64
MLA_WIDTH = MLA_HEADS * MLA_V

SSD_HEADS = 4
SSD_HEAD_DIM = 64
SSD_INNER = SSD_HEADS * SSD_HEAD_DIM
SSD_GROUPS = 2
SSD_STATE = 128
SSD_CONV_CH = SSD_INNER + 2 * SSD_GROUPS * SSD_STATE

D_FF = 128 * ((8 * D_MODEL // 3 + 127) // 128)

IN_SIZES = (3 * GDN_WIDTH, GDN_WIDTH, 2 * GDN_HEADS, 2 * GDN_HEADS,
            MLA_Q_LORA, MLA_KV_LORA, MLA_ROPE,
            SSD_INNER, SSD_CONV_CH, 2 * SSD_HEADS)
IN_OFFSETS = tuple(int(o) for o in np.cumsum((0,) + IN_SIZES)[:-1])

LANES = 128
SUBLANES = 8
VMEM_LIMIT_BYTES = 56 * 1024 * 1024

SM_KR = 0
SM_GA = MLA_ROPE
SM_GB = SM_GA + 2 * GDN_HEADS
SM_DT = SM_GB + 2 * GDN_HEADS
PC_QKV = 0
PC_GATE = PC_QKV + 3 * GDN_WIDTH
PC_CQ = PC_GATE + GDN_WIDTH
PC_CKV = PC_CQ + MLA_Q_LORA
PC_Z = PC_CKV + MLA_KV_LORA
PC_XBC = PC_Z + SSD_INNER
PC_SMALL = PC_XBC + SSD_CONV_CH
PC_TOTAL = PC_SMALL + LANES
HEAD_BLOCK = LANES
SCAN_BLOCK = 512
HALO = SUBLANES


def _bdot(a, b):
    return jnp.dot(a.astype(BF16), b.astype(BF16), preferred_element_type=F32)


def _bdot_nt(a, b):
    return lax.dot_general(a.astype(BF16), b.astype(BF16), (((1,), (1,)), ((), ())),
                           preferred_element_type=F32)


def _bdot_tn(a, b):
    return lax.dot_general(a.astype(BF16), b.astype(BF16), (((0,), (0,)), ((), ())),
                           preferred_element_type=F32)


def _fdot(a, b):
    return jnp.dot(a, b, precision=lax.Precision.HIGHEST, preferred_element_type=F32)


def _silu(x):
    return x * (1.0 / (1.0 + jnp.exp(-x)))


def _sigmoid(x):
    return 1.0 / (1.0 + jnp.exp(-x))


def _softplus(x):
    return jnp.maximum(x, 0.0) + jnp.log(1.0 + jnp.exp(-jnp.abs(x)))


def _rms(x, gain):
    ms = jnp.mean(x * x, axis=-1, keepdims=True)
    return x * lax.rsqrt(ms + EPS) * gain


def _params(*sem):
    return pltpu.CompilerParams(dimension_semantics=sem, vmem_limit_bytes=VMEM_LIMIT_BYTES)


def _const_spec(shape):
    nd = len(shape)
    return pl.BlockSpec(shape, lambda *_: (0,) * nd, pipeline_mode=pl.Buffered(1))


def _ada_kernel(cond_ref, w_ref, b_ref, o_ref):
    a = _silu(cond_ref[...])
    o_ref[...] = jnp.dot(a, w_ref[...], precision=lax.Precision.HIGHEST,
                         preferred_element_type=F32) + b_ref[...]


def _ada(cond8, w_ada, b_ada):
    n_col = 6 * D_MODEL // D_MODEL
    return pl.pallas_call(
        _ada_kernel,
        out_shape=jax.ShapeDtypeStruct((DEPTH, SUBLANES, 6 * D_MODEL), F32),
        grid=(DEPTH, n_col),
        in_specs=[pl.BlockSpec((SUBLANES, D_MODEL), lambda l, j: (0, 0)),
                  pl.BlockSpec((None, D_MODEL, D_MODEL), lambda l, j: (l, 0, j)),
                  pl.BlockSpec((None, 1, D_MODEL), lambda l, j: (l, 0, j))],
        out_specs=pl.BlockSpec((None, SUBLANES, D_MODEL), lambda l, j: (l, 0, j)),
        compiler_params=_params("arbitrary", "arbitrary"),
        name="ada",
    )(cond8, w_ada, b_ada.reshape(DEPTH, 1, 6 * D_MODEL))


def _rope_lanes(x, c, s_up, s_dn):
    half = MLA_ROPE // 2
    return (x * c + pltpu.roll(x, half, 1) * s_up + pltpu.roll(x, LANES - half, 1) * s_dn)


def _inproj_kernel(use_rope, *refs):
    if use_rope:
        (x_ref, mod_ref, gpre_ref, win_ref, qn_ref, wuq_ref, kvn_ref, wk_ref, wv_ref, pk_ref,
         rope_ref, qkv_o, gate_o, qm_o, ckv_o, kmat_o, vmat_o, z_o, xbc_o, small_o) = refs
    else:
        (x_ref, mod_ref, gpre_ref, win_ref, qn_ref, wuq_ref, kvn_ref, wk_ref, wv_ref, pk_ref,
         qkv_o, gate_o, qm_o, ckv_o, kmat_o, vmat_o, z_o, xbc_o, small_o) = refs
    x = x_ref[...]
    sh1 = mod_ref[:, 0:D_MODEL]
    sc1 = mod_ref[:, D_MODEL:2 * D_MODEL]
    h = _rms(x, gpre_ref[...]) * (1.0 + sc1) + sh1
    proj = _bdot(h, win_ref[...])
    qkv_o[...] = proj[:, PC_QKV:PC_GATE]
    gate_o[...] = proj[:, PC_GATE:PC_CQ]
    z_o[...] = proj[:, PC_Z:PC_XBC]
    xbc_o[...] = proj[:, PC_XBC:PC_SMALL]
    small = proj[:, PC_SMALL:PC_TOTAL]
    small_o[...] = small

    cq = _rms(proj[:, PC_CQ:PC_CKV], qn_ref[...])
    qm = _bdot(cq, wuq_ref[...]) * ((MLA_NOPE + MLA_ROPE) ** -0.5 * LOG2_E)
    ckv = _rms(proj[:, PC_CKV:PC_Z], kvn_ref[...])
    ckv_o[...] = ckv
    if use_rope:
        cq_t = rope_ref[:, 0:LANES]
        s1q_t = rope_ref[:, LANES:2 * LANES]
        s2q_t = rope_ref[:, 2 * LANES:3 * LANES]
        for hd in range(MLA_HEADS):
            blk = qm[:, hd * HEAD_BLOCK:(hd + 1) * HEAD_BLOCK]
            qm_o[:, hd * HEAD_BLOCK:(hd + 1) * HEAD_BLOCK] = _rope_lanes(
                blk, cq_t, s1q_t, s2q_t).astype(BF16)
        kr = _rope_lanes(small, rope_ref[:, 3 * LANES:4 * LANES],
                         rope_ref[:, 4 * LANES:5 * LANES], rope_ref[:, 5 * LANES:6 * LANES])
    else:
        qm_o[...] = qm.astype(BF16)
        kr = small
    kmat_o[...] = (_bdot(ckv, wk_ref[...]) + _bdot(kr, pk_ref[...])).astype(BF16)
    vmat_o[...] = _bdot_nt(wv_ref[...], ckv).astype(BF16)


def _inproj(x, mod, per_batch_mod, lw, rope_tab, tm):
    bsz, length, _ = x.shape
    use_rope = rope_tab is not None
    nb = length // tm
    tok = lambda w: pl.BlockSpec((None, tm, w), lambda b, i: (b, i, 0))
    mod_map = (lambda b, i: (b, 0, 0)) if per_batch_mod else (lambda b, i: (0, 0, 0))
    in_specs = [tok(D_MODEL),
                pl.BlockSpec((None, 1, 6 * D_MODEL), mod_map),
                _const_spec((1, D_MODEL)),
                _const_spec((D_MODEL, PC_TOTAL)),
                _const_spec((1, MLA_Q_LORA)),
                _const_spec((MLA_Q_LORA, MLA_HEADS * HEAD_BLOCK)),
                _const_spec((1, MLA_KV_LORA)),
                _const_spec((MLA_KV_LORA, MLA_HEADS * HEAD_BLOCK)),
                _const_spec((MLA_WIDTH, MLA_KV_LORA)),
                _const_spec((LANES, MLA_HEADS * HEAD_BLOCK))]
    args = [x, mod, lw['norm_mix_pre'], lw['w_in'], lw['mla_q_norm'], lw['w_uq'],
            lw['mla_kv_norm'], lw['w_k'], lw['w_vt'], lw['p_k']]
    if use_rope:
        in_specs.append(pl.BlockSpec((tm, 6 * LANES), lambda b, i: (i, 0)))
        args.append(rope_tab)
    widths = [(3 * GDN_WIDTH, F32), (GDN_WIDTH, F32), (MLA_HEADS * HEAD_BLOCK, BF16),
              (MLA_KV_LORA, F32), (MLA_HEADS * HEAD_BLOCK, BF16), None,
              (SSD_INNER, F32), (SSD_CONV_CH, F32), (LANES, F32)]
    out_shape = [jax.ShapeDtypeStruct((bsz, MLA_WIDTH, length), BF16) if w is None
                 else jax.ShapeDtypeStruct((bsz, length, w[0]), w[1]) for w in widths]
    out_specs = [pl.BlockSpec((None, MLA_WIDTH, tm), lambda b, i: (b, 0, i)) if w is None
                 else tok(w[0]) for w in widths]
    return pl.pallas_call(
        functools.partial(_inproj_kernel, use_rope),
        out_shape=out_shape,
        grid=(bsz, nb),
        in_specs=in_specs,
        out_specs=out_specs,
        compiler_params=_params("parallel", "parallel"),
        name="inproj",
    )(*args)


def _ctxkv_kernel(ckv_ref, kr_ref, wk_ref, wv_ref, pk_ref, kmat_o, vmat_o):
    ckv = ckv_ref[...]
    kmat_o[...] = (_bdot(ckv, wk_ref[...]) + _bdot(kr_ref[...], pk_ref[...])).astype(BF16)
    vmat_o[...] = _bdot_nt(wv_ref[...], ckv).astype(BF16)


def _ctxkv(ckv, kr_pad, lw):
    bsz, length, _ = ckv.shape
    tok = lambda w: pl.BlockSpec((None, length, w), lambda b: (b, 0, 0))
    return pl.pallas_call(
        _ctxkv_kernel,
        out_shape=[jax.ShapeDtypeStruct((bsz, length, MLA_HEADS * HEAD_BLOCK), BF16),
                   jax.ShapeDtypeStruct((bsz, MLA_WIDTH, length), BF16)],
        grid=(bsz,),
        in_specs=[tok(MLA_KV_LORA), tok(LANES),
                  _const_spec((MLA_KV_LORA, MLA_HEADS * HEAD_BLOCK)),
                  _const_spec((MLA_WIDTH, MLA_KV_LORA)),
                  _const_spec((LANES, MLA_HEADS * HEAD_BLOCK))],
        out_specs=[tok(MLA_HEADS * HEAD_BLOCK),
                   pl.BlockSpec((None, MLA_WIDTH, length), lambda b: (b, 0, 0))],
        compiler_params=_params("parallel"),
        name="ctxkv",
    )(ckv, kr_pad, lw['w_k'], lw['w_vt'], lw['p_k'])


ATTN_HEADS = 8
ATTN_KEY_CHUNK = 1024


def _attn_kernel(n_seg, *refs):
    q_ref = refs[0]
    k_refs = refs[1:1 + n_seg]
    v_refs = refs[1 + n_seg:1 + 2 * n_seg]
    o_ref = refs[1 + 2 * n_seg]
    chunks = []
    for seg, k in enumerate(k_refs):
        for r0 in range(0, k.shape[0], ATTN_KEY_CHUNK):
            chunks.append((seg, r0, min(ATTN_KEY_CHUNK, k.shape[0] - r0)))

    def scores(hh, chunk):
        seg, r0, n = chunk
        s = lax.dot_general(k_refs[seg][r0:r0 + n, hh * HEAD_BLOCK:(hh + 1) * HEAD_BLOCK],
                            q_ref[:, hh * HEAD_BLOCK:(hh + 1) * HEAD_BLOCK],
                            (((1,), (1,)), ((), ())), preferred_element_type=F32)
        return s, s.max(axis=0, keepdims=True)

    def weigh(hh, chunk, s, m):
        seg, r0, n = chunk
        p = jnp.exp2(s - m)
        pv = jnp.dot(v_refs[seg][hh * MLA_V:(hh + 1) * MLA_V, r0:r0 + n], p.astype(BF16),
                     preferred_element_type=F32)
        return pv, p.sum(axis=0, keepdims=True)

    outs = []
    nxt = [scores(0, c) for c in chunks]
    for hh in range(ATTN_HEADS):
        cur = nxt
        nxt = []
        m = cur[0][1]
        for _, mc in cur[1:]:
            m = jnp.maximum(m, mc)
        acc = None
        den = None
        for ci, c in enumerate(chunks):
            if hh + 1 < ATTN_HEADS:
                nxt.append(scores(hh + 1, c))
            pv, d = weigh(hh, c, cur[ci][0], m)
            acc = pv if acc is None else acc + pv
            den = d if den is None else den + d
        outs.append(acc / den)
    o_ref[...] = jnp.concatenate(outs, axis=0).T.astype(BF16)


def _attention(qm, kmats, vmats, tq):
    bsz, length, _ = qm.shape
    n_seg = len(kmats)
    n_grp = MLA_HEADS // ATTN_HEADS
    in_specs = [pl.BlockSpec((None, tq, ATTN_HEADS * HEAD_BLOCK), lambda b, p, i: (b, i, p))]
    for k in kmats:
        in_specs.append(pl.BlockSpec((None, k.shape[1], ATTN_HEADS * HEAD_BLOCK),
                                     lambda b, p, i: (b, 0, p)))
    for v in vmats:
        in_specs.append(pl.BlockSpec((None, ATTN_HEADS * MLA_V, v.shape[2]),
                                     lambda b, p, i: (b, p, 0)))
    return pl.pallas_call(
        functools.partial(_attn_kernel, n_seg),
        out_shape=jax.ShapeDtypeStruct((bsz, length, MLA_WIDTH), BF16),
        grid=(bsz, n_grp, length // tq),
        in_specs=in_specs,
        out_specs=pl.BlockSpec((None, tq, ATTN_HEADS * MLA_V), lambda b, p, i: (b, i, p)),
        compiler_params=_params("parallel", "parallel", "arbitrary"),
        name="attn",
    )(qm, *kmats, *vmats)


def _conv_tile(xe_ref, prev_ref, main_ref, next_ref, w_ref, blk, n_blk, tb):
    xe_ref[0:HALO, :] = jnp.where(blk > 0, prev_ref[...], 0.0)
    xe_ref[HALO:HALO + tb, :] = main_ref[...]
    xe_ref[HALO + tb:HALO + tb + HALO, :] = jnp.where(blk < n_blk - 1, next_ref[...], 0.0)
    xe = xe_ref[...]
    rows = tb + 2 * HALO
    acc = None
    for j in range(CONV_K):
        shift = (CONV_K // 2 - j) % rows
        tap = xe if shift == 0 else pltpu.roll(xe, shift, 0)
        term = tap * w_ref[j:j + 1, :]
        acc = term if acc is None else acc + term
    return acc[HALO:HALO + tb, :]


def _chunk_tri(tb, reverse):
    r = lax.broadcasted_iota(jnp.int32, (tb, tb), 0)
    c = lax.broadcasted_iota(jnp.int32, (tb, tb), 1)
    same = (r // CHUNK) == (c // CHUNK)
    order = (c >= r) if reverse else (c <= r)
    return jnp.where(same & order, 1.0, 0.0).astype(BF16)


def _masks(reverse, width):
    r = lax.broadcasted_iota(jnp.int32, (CHUNK, width), 0)
    c = lax.broadcasted_iota(jnp.int32, (CHUNK, width), 1) % CHUNK
    eye = jnp.where(c == r, 1.0, 0.0).astype(F32)
    if reverse:
        return c >= r, c > r, eye
    return c <= r, c < r, eye


def _lane_block(shape, per):
    return lax.broadcasted_iota(jnp.int32, shape, 1) // per


def _bd(x, n_blk):
    blk = _lane_block(x.shape, x.shape[1] // n_blk)
    return jnp.concatenate([jnp.where(blk == h, x, 0.0) for h in range(n_blk)], axis=0)


def _split2(x):
    hi = x.astype(BF16).astype(F32)
    return hi, x - hi


def _split3(x):
    hi, r1 = _split2(x)
    mid, lo = _split2(r1)
    return hi.astype(BF16), mid.astype(BF16), lo.astype(BF16)


def _sel_dot(x, sel):
    return jnp.dot(jnp.concatenate(_split3(x), axis=1), jnp.concatenate([sel, sel, sel], axis=0),
                   preferred_element_type=F32)


def _tri_dot(tri, x):
    return jnp.dot(jnp.concatenate([tri, tri, tri], axis=1), jnp.concatenate(_split3(x), axis=0),
                   preferred_element_type=F32)


def _lane_select(rows, first, width, per):
    r = lax.broadcasted_iota(jnp.int32, (rows, width), 0)
    c = lax.broadcasted_iota(jnp.int32, (rows, width), 1)
    return jnp.where(r == first + c // per, 1.0, 0.0).astype(BF16)


def _scan_specs(bsz, length, width, tb, reverse):
    n_blk = length // tb
    per = tb // HALO
    n_halo = length // HALO
    blk_of = (lambda i: n_blk - 1 - i) if reverse else (lambda i: i)
    main = pl.BlockSpec((None, tb, width), lambda b, i: (b, blk_of(i), 0))
    prev = pl.BlockSpec((None, HALO, width),
                        lambda b, i: (b, jnp.maximum(blk_of(i) * per - 1, 0), 0))
    nxt = pl.BlockSpec((None, HALO, width),
                       lambda b, i: (b, jnp.minimum((blk_of(i) + 1) * per, n_halo - 1), 0))
    small = pl.BlockSpec((None, tb, LANES), lambda b, i: (b, blk_of(i), 0))
    return n_blk, blk_of, main, prev, nxt, small


def _gdn_body(reverse, tb, n_blk, refs):
    if reverse:
        qkv_s, small_ref, prow_ref, s0_ref, o_ref, sfin_ref, g_s, b_s, state = refs
    else:
        (main_ref, prev_ref, next_ref, small_ref, w_ref, prow_ref, s0_ref,
         o_ref, sfin_ref, qkv_s, xe_ref, g_s, b_s, state) = refs
    i = pl.program_id(1)
    d = 1 if reverse else 0

    @pl.when(i == 0)
    def _():
        state[...] = s0_ref[...]

    nh = GDN_HEADS
    if not reverse:
        qkv = _silu(_conv_tile(xe_ref, prev_ref, main_ref, next_ref, w_ref, i, n_blk, tb))
        q = qkv[:, 0:GDN_WIDTH]
        k = qkv[:, GDN_WIDTH:2 * GDN_WIDTH]
        head_ones = jnp.where(
            lax.broadcasted_iota(jnp.int32, (GDN_WIDTH, GDN_WIDTH), 0) // GDN_DK
            == _lane_block((GDN_WIDTH, GDN_WIDTH), GDN_DK), 1.0, 0.0).astype(BF16)
        qkv_s[:, 0:GDN_WIDTH] = q * lax.rsqrt(_sel_dot(q * q, head_ones) + EPS) * (GDN_DK ** -0.5)
        qkv_s[:, GDN_WIDTH:2 * GDN_WIDTH] = k * lax.rsqrt(_sel_dot(k * k, head_ones) + EPS)
        qkv_s[:, 2 * GDN_WIDTH:3 * GDN_WIDTH] = qkv[:, 2 * GDN_WIDTH:3 * GDN_WIDTH]

    yield
    small = small_ref[...]
    log_a = -jnp.exp(prow_ref[0:1, :]) * _softplus(small + prow_ref[1:2, :])
    packed = jnp.where(lax.broadcasted_iota(jnp.int32, small.shape, 1) < SM_GB, log_a,
                       _sigmoid(small))
    sel = jnp.concatenate([_lane_select(LANES, SM_GA + d * nh, GDN_WIDTH, GDN_DK),
                           _lane_select(LANES, SM_GB + d * nh, GDN_WIDTH, GDN_DK)], axis=1)
    spread = _sel_dot(packed, sel)
    g_s[...] = _tri_dot(_chunk_tri(tb, reverse), spread[:, 0:GDN_WIDTH])
    b_s[...] = spread[:, GDN_WIDTH:2 * GDN_WIDTH]

    yield
    incl, strict, eye = _masks(reverse, GDN_WIDTH)
    n_chunk = tb // CHUNK
    last = 0 if reverse else CHUNK - 1
    chunks = range(n_chunk)
    row_sl = [slice(c * CHUNK, (c + 1) * CHUNK) for c in chunks]

    g_col = [g_s[r, :] for r in row_sl]
    beta = [b_s[r, :] for r in row_sl]
    g_last = [g[last:last + 1, :] for g in g_col]
    decay = []
    for g in g_col:
        g_row = jnp.sum(eye * g, axis=0, keepdims=True)
        decay.append(jnp.where(incl, jnp.exp(jnp.where(incl, g - g_row, 0.0)), 0.0))
    q4 = [qkv_s[r, 0:GDN_WIDTH] for r in row_sl]
    k4 = [qkv_s[r, GDN_WIDTH:2 * GDN_WIDTH] for r in row_sl]
    qk_kk = [_bdot_nt(jnp.concatenate([q4[c], k4[c]], axis=0), _bd(k4[c], nh)) for c in chunks]
    m_pow = [-jnp.where(strict, qk_kk[c][CHUNK:2 * CHUNK, :] * decay[c], 0.0) * beta[c]
             for c in chunks]
    yield
    t_inv = [eye + m for m in m_pow]
    n_sq = 6
    for step in range(n_sq):
        for c in chunks:
            m_hi, m_lo = _split2(m_pow[c])
            if step == 0:
                lhs_f = [(m_hi, m_lo)]
            elif step == n_sq - 1:
                lhs_f = [_split2(t_inv[c])]
            else:
                lhs_f = [(m_hi, m_lo), _split2(t_inv[c])]
            hi = jnp.concatenate([p[0] for p in lhs_f], axis=0).astype(BF16)
            lo = jnp.concatenate([p[1] for p in lhs_f], axis=0).astype(BF16)
            n_rows = hi.shape[0]
            both = jnp.dot(jnp.concatenate([hi, lo], axis=0), _bd(m_hi, nh).astype(BF16),
                           preferred_element_type=F32)
            prod = (both[0:n_rows, :] + both[n_rows:2 * n_rows, :]
                    + jnp.dot(hi, _bd(m_lo, nh).astype(BF16), preferred_element_type=F32))
            if step == 0:
                m_pow[c] = prod
            elif step == n_sq - 1:
                t_inv[c] = t_inv[c] + prod
            else:
                t_inv[c] = t_inv[c] + prod[CHUNK:2 * CHUNK, :]
                m_pow[c] = prod[0:CHUNK, :]
        yield
    uw, attn, q_dec, k_dec = [], [], [], []
    for c in chunks:
        eg = jnp.exp(g_col[c])
        v4 = qkv_s[row_sl[c], 2 * GDN_WIDTH:3 * GDN_WIDTH]
        uw.append(_bdot(t_inv[c], jnp.concatenate(
            [_bd(v4 * beta[c], nh), _bd(k4[c] * (beta[c] * eg), nh)], axis=1)))
        attn.append(jnp.where(incl, qk_kk[c][0:CHUNK, :] * decay[c], 0.0))
        q_dec.append(q4[c] * eg)
        k_dec.append(k4[c] * jnp.exp(g_last[c] - g_col[c]))

    yield
    blk_id = _lane_block((GDN_DK, GDN_WIDTH), GDN_DV)

    def head_blocks(x):
        out = None
        for hd in range(nh):
            part = jnp.where(blk_id == hd, x[hd * GDN_DK:(hd + 1) * GDN_DK, :], 0.0)
            out = part if out is None else out + part
        return out

    k_u, k_w = [], []
    for c in chunks:
        kuw = _bdot_tn(k_dec[c], uw[c])
        k_u.append(head_blocks(kuw[:, 0:GDN_WIDTH]))
        k_w.append(head_blocks(kuw[:, GDN_WIDTH:2 * GDN_WIDTH]))

    yield

    def emit_output(c, s_in):
        ws_qs = _bdot(jnp.concatenate([uw[c][:, GDN_WIDTH:2 * GDN_WIDTH], q_dec[c]], axis=0),
                      _bd(s_in, nh))
        v_new = uw[c][:, 0:GDN_WIDTH] - ws_qs[0:CHUNK, :]
        o_ref[row_sl[c], :] = ws_qs[CHUNK:2 * CHUNK, :] + _bdot(attn[c], _bd(v_new, nh))

    s4 = state[...]
    prev = None
    for ci in chunks:
        c = (n_chunk - 1 - ci) if reverse else ci
        s_in = s4
        s4 = s4 * jnp.exp(g_last[c]) + k_u[c] - _bdot(k_w[c], _bd(s4, nh))
        if prev is not None:
            emit_output(*prev)
        prev = (c, s_in)
    emit_output(*prev)
    state[...] = s4
    sfin_ref[...] = s4


def _gdn(qkv, small, conv_w, prow, s0, reverse, tb):
    bsz, length, width = qkv.shape
    n_blk, blk_of, main, prev, nxt, sm = _scan_specs(bsz, length, width, tb, reverse)
    st = pl.BlockSpec((None, GDN_DK, GDN_WIDTH), lambda b, i: (b, 0, 0))
    s0 = s0.transpose(0, 2, 1, 3).reshape(bsz, GDN_DK, GDN_WIDTH)
    out_shape = [jax.ShapeDtypeStruct((bsz, length, GDN_WIDTH), F32),
                 jax.ShapeDtypeStruct((bsz, GDN_DK, GDN_WIDTH), F32)]
    out_specs = [pl.BlockSpec((None, tb, GDN_WIDTH), lambda b, i: (b, blk_of(i), 0)), st]
    scratch = [pltpu.VMEM((tb, GDN_WIDTH), F32), pltpu.VMEM((tb, GDN_WIDTH), F32),
               pltpu.VMEM((GDN_DK, GDN_WIDTH), F32)]
    if reverse:
        in_specs = [main, sm, _const_spec((SUBLANES, LANES)), st]
        args = (qkv, small, prow, s0)
    else:
        in_specs = [main, prev, nxt, sm, _const_spec((SUBLANES, width)),
                    _const_spec((SUBLANES, LANES)), st]
        args = (qkv, qkv, qkv, small, conv_w, prow, s0)
        out_shape.append(jax.ShapeDtypeStruct((bsz, length, width), F32))
        out_specs.append(main)
        scratch = [pltpu.VMEM((tb + 2 * HALO, width), F32)] + scratch

    def finish(res):
        s_fin = res[1].reshape(bsz, GDN_DK, GDN_HEADS, GDN_DV).transpose(0, 2, 1, 3)
        return (res[0], s_fin) + tuple(res[2:])

    return dict(body=_gdn_body, in_specs=in_specs, args=args, out_shape=out_shape,
                out_specs=out_specs, scratch=scratch, finish=finish)


def _ssd_body(reverse, tb, n_blk, refs):
    if reverse:
        xbc_s, small_ref, prow_ref, s0_ref, y_ref, sfin_ref, dt_s, a_s, state = refs
    else:
        (main_ref, prev_ref, next_ref, small_ref, w_ref, cb_ref, prow_ref, dvec_ref, s0_ref,
         y_ref, sfin_ref, xbc_s, xe_ref, dt_s, a_s, state) = refs
    i = pl.program_id(1)
    d = 1 if reverse else 0

    @pl.when(i == 0)
    def _():
        state[...] = s0_ref[...]

    if not reverse:
        xbc_s[...] = _silu(_conv_tile(xe_ref, prev_ref, main_ref, next_ref, w_ref, i, n_blk, tb)
                           + cb_ref[...])
    b_lo = SSD_INNER
    c_lo = SSD_INNER + SSD_GROUPS * SSD_STATE
    yield
    dt = _softplus(small_ref[...] + prow_ref[1:2, :])
    sel = _lane_select(LANES, SM_DT + d * SSD_HEADS, SSD_INNER, SSD_HEAD_DIM)
    spread = _sel_dot(jnp.concatenate([dt * (-jnp.exp(prow_ref[0:1, :])), dt], axis=0), sel)
    a_s[...] = _tri_dot(_chunk_tri(tb, reverse), spread[0:tb, :])
    dt_s[...] = spread[tb:2 * tb, :]

    yield
    incl, _, eye = _masks(reverse, SSD_INNER)
    n_chunk = tb // CHUNK
    rep = SSD_HEADS // SSD_GROUPS
    last = 0 if reverse else CHUNK - 1
    grp_of_lane = _lane_block((CHUNK, SSD_GROUPS * SSD_STATE), SSD_STATE)
    grp_of_lane_s = _lane_block((SSD_STATE, SSD_INNER), rep * SSD_HEAD_DIM)
    for ci in range(n_chunk):
        cc = (n_chunk - 1 - ci) if reverse else ci
        rows = slice(cc * CHUNK, (cc + 1) * CHUNK)
        a_col = a_s[rows, :]
        a_row = jnp.sum(eye * a_col, axis=0, keepdims=True)
        a_last = a_col[last:last + 1, :]
        lmat = jnp.where(incl, jnp.exp(jnp.where(incl, a_col - a_row, 0.0)), 0.0)
        xs = xbc_s[rows, 0:b_lo]
        bm = xbc_s[rows, b_lo:c_lo]
        cm = xbc_s[rows, c_lo:SSD_CONV_CH]
        xdt = xs * dt_s[rows, :]
        b_heads = jnp.concatenate(
            [jnp.where(grp_of_lane == hd // rep, bm, 0.0) for hd in range(SSD_HEADS)], axis=0)
        scores = _bdot_nt(cm, b_heads) * lmat
        st = state[...]
        s_groups = jnp.concatenate(
            [jnp.where(grp_of_lane_s == g, st, 0.0) for g in range(SSD_GROUPS)], axis=0)
        y = _bdot(scores, _bd(xdt, SSD_HEADS)) + _bdot(cm, s_groups) * jnp.exp(a_col)
        if not reverse:
            y = y + xs * dvec_ref[...]
        y_ref[rows, :] = y
        cs = _bdot_tn(bm, xdt * jnp.exp(a_last - a_col))
        upd = jnp.where(grp_of_lane_s == 0, cs[0:SSD_STATE, :], cs[SSD_STATE:2 * SSD_STATE, :])
        state[...] = st * jnp.exp(a_last) + upd
        yield
    sfin_ref[...] = state[...]


def _ssd(xbc, small, conv_w, conv_b, prow, dvec, s0, reverse, tb):
    bsz, length, width = xbc.shape
    n_blk, blk_of, main, prev, nxt, sm = _scan_specs(bsz, length, width, tb, reverse)
    st = pl.BlockSpec((None, SSD_STATE, SSD_INNER), lambda b, i: (b, 0, 0))
    s0 = s0.transpose(0, 3, 1, 2).reshape(bsz, SSD_STATE, SSD_INNER)
    out_shape = [jax.ShapeDtypeStruct((bsz, length, SSD_INNER), F32),
                 jax.ShapeDtypeStruct((bsz, SSD_STATE, SSD_INNER), F32)]
    out_specs = [pl.BlockSpec((None, tb, SSD_INNER), lambda b, i: (b, blk_of(i), 0)), st]
    scratch = [pltpu.VMEM((tb, SSD_INNER), F32), pltpu.VMEM((tb, SSD_INNER), F32),
               pltpu.VMEM((SSD_STATE, SSD_INNER), F32)]
    if reverse:
        in_specs = [main, sm, _const_spec((SUBLANES, LANES)), st]
        args = (xbc, small, prow, s0)
    else:
        in_specs = [main, prev, nxt, sm, _const_spec((SUBLANES, width)), _const_spec((1, width)),
                    _const_spec((SUBLANES, LANES)), _const_spec((1, SSD_INNER)), st]
        args = (xbc, xbc, xbc, small, conv_w, conv_b, prow, dvec, s0)
        out_shape.append(jax.ShapeDtypeStruct((bsz, length, width), F32))
        out_specs.append(main)
        scratch = [pltpu.VMEM((tb + 2 * HALO, width), F32)] + scratch

    def finish(res):
        s_fin = res[1].reshape(bsz, SSD_STATE, SSD_HEADS, SSD_HEAD_DIM).transpose(0, 2, 3, 1)
        return (res[0], s_fin) + tuple(res[2:])

    return dict(body=_ssd_body, in_specs=in_specs, args=args, out_shape=out_shape,
                out_specs=out_specs, scratch=scratch, finish=finish)


def _scan_kernel(reverse, tb, n_blk, counts, *refs):
    n_in, n_out, n_scr = counts
    ins = refs[:sum(n_in)]
    outs = refs[sum(n_in):sum(n_in) + sum(n_out)]
    scr = refs[sum(n_in) + sum(n_out):]
    gdn_refs = ins[:n_in[0]] + outs[:n_out[0]] + scr[:n_scr[0]]
    ssd_refs = ins[n_in[0]:] + outs[n_out[0]:] + scr[n_scr[0]:]
    for _ in itertools.zip_longest(_gdn_body(reverse, tb, n_blk, gdn_refs),
                                   _ssd_body(reverse, tb, n_blk, ssd_refs)):
        pass


def _scan(gdn, ssd, bsz, n_blk, reverse, tb):
    counts = tuple((len(gdn[k]), len(ssd[k])) for k in ('in_specs', 'out_shape', 'scratch'))
    res = pl.pallas_call(
        functools.partial(_scan_kernel, reverse, tb, n_blk, counts),
        out_shape=gdn['out_shape'] + ssd['out_shape'],
        grid=(bsz, n_blk),
        in_specs=gdn['in_specs'] + ssd['in_specs'],
        out_specs=gdn['out_specs'] + ssd['out_specs'],
        scratch_shapes=gdn['scratch'] + ssd['scratch'],
        compiler_params=_params("parallel", "arbitrary"),
        name="scan_bwd" if reverse else "scan_fwd",
    )(*gdn['args'], *ssd['args'])
    n_g = len(gdn['out_shape'])
    return gdn['finish'](res[:n_g]), ssd['finish'](res[n_g:])


def _outproj_kernel(x_ref, mod_ref, of_ref, ob_ref, gate_ref, om_ref, yf_ref, yb_ref, z_ref,
                    gn_ref, sn_ref, wout_ref, gpost_ref, o_ref):
    og = of_ref[...] + ob_ref[...]
    parts = []
    for hd in range(GDN_HEADS):
        lo = hd * GDN_DV
        oh = og[:, lo:lo + GDN_DV]
        parts.append(oh * lax.rsqrt(jnp.mean(oh * oh, axis=-1, keepdims=True) + EPS))
    o_gdn = jnp.concatenate(parts, axis=-1) * gn_ref[...] * _silu(gate_ref[...])
    o_ssd = _rms((yf_ref[...] + yb_ref[...]) * _silu(z_ref[...]), sn_ref[...])
    mixed = (_bdot(o_gdn, wout_ref[0:GDN_WIDTH, :])
             + jnp.dot(om_ref[...], wout_ref[GDN_WIDTH:GDN_WIDTH + MLA_WIDTH, :],
                       preferred_element_type=F32)
             + _bdot(o_ssd, wout_ref[GDN_WIDTH + MLA_WIDTH:, :]))
    g1 = mod_ref[:, 2 * D_MODEL:3 * D_MODEL]
    o_ref[...] = x_ref[...] + g1 * _rms(mixed, gpost_ref[...])


def _outproj(x, mod, per_batch_mod, o_f, o_b, gate, o_mla, y_f, y_b, z, lw, tm):
    bsz, length, _ = x.shape
    tok = lambda w: pl.BlockSpec((None, tm, w), lambda b, i: (b, i, 0))
    mod_map = (lambda b, i: (b, 0, 0)) if per_batch_mod else (lambda b, i: (0, 0, 0))
    return pl.pallas_call(
        _outproj_kernel,
        out_shape=jax.ShapeDtypeStruct((bsz, length, D_MODEL), F32),
        grid=(bsz, length // tm),
        in_specs=[tok(D_MODEL), pl.BlockSpec((None, 1, 6 * D_MODEL), mod_map),
                  tok(GDN_WIDTH), tok(GDN_WIDTH), tok(GDN_WIDTH), tok(MLA_WIDTH),
                  tok(SSD_INNER), tok(SSD_INNER), tok(SSD_INNER),
                  _const_spec((1, GDN_WIDTH)), _const_spec((1, SSD_INNER)),
                  _const_spec((D_MODEL, D_MODEL)), _const_spec((1, D_MODEL))],
        out_specs=tok(D_MODEL),
        compiler_params=_params("parallel", "parallel"),
        name="outproj",
    )(x, mod, o_f, o_b, gate, o_mla, y_f, y_b, z, lw['gdn_norm'], lw['ssd_norm'], lw['w_out'],
      lw['norm_mix_post'])


FFN_COLS = 256
FFN_ROWS = 1024


def _ffn_kernel(tm, n_blk, main_ref, prev_ref, next_ref, mod_ref, gpre_ref, wup_ref, cw_ref,
                wdn_ref, gpost_ref, o_ref, xe_ref):
    i = pl.program_id(1)
    rows = tm + 2 * HALO
    xe_ref[0:HALO, :] = prev_ref[...]
    xe_ref[HALO:HALO + tm, :] = main_ref[...]
    xe_ref[HALO + tm:rows, :] = next_ref[...]
    sh2 = mod_ref[:, 3 * D_MODEL:4 * D_MODEL]
    sc2 = mod_ref[:, 4 * D_MODEL:5 * D_MODEL]
    g2 = mod_ref[:, 5 * D_MODEL:6 * D_MODEL]
    h = _rms(xe_ref[...], gpre_ref[...]) * (1.0 + sc2) + sh2
    r = lax.broadcasted_iota(jnp.int32, (rows, 1), 0)
    valid = ((r >= HALO) | (i > 0)) & ((r < HALO + tm) | (i < n_blk - 1))
    hb = jnp.where(valid, h, 0.0).astype(BF16)

    def conv3(up, col0):
        acc = None
        for j in range(FFN_CONV_K):
            shift = (FFN_CONV_K // 2 - j) % rows
            tap = up if shift == 0 else pltpu.roll(up, shift, 0)
            term = tap * cw_ref[j:j + 1, col0:col0 + FFN_COLS]
            acc = term if acc is None else acc + term
        return acc[HALO:HALO + tm, :]

    def up(cb):
        c0 = cb * FFN_COLS
        return (jnp.dot(hb, wup_ref[:, c0:c0 + FFN_COLS], preferred_element_type=F32),
                jnp.dot(hb, wup_ref[:, D_FF + c0:D_FF + c0 + FFN_COLS],
                        preferred_element_type=F32))

    n_slab = D_FF // FFN_COLS
    acc = None
    ahead = 2
    pending = {cb: up(cb) for cb in range(ahead)}
    for cb in range(n_slab):
        c0 = cb * FFN_COLS
        g_up, u_up = pending.pop(cb)
        if cb + ahead < n_slab:
            pending[cb + ahead] = up(cb + ahead)
        act = _silu(conv3(g_up, c0)) * conv3(u_up, D_FF + c0)
        part = _bdot(act, wdn_ref[c0:c0 + FFN_COLS, :])
        acc = part if acc is None else acc + part
    o_ref[...] = main_ref[...] + g2 * _rms(acc, gpost_ref[...])


def _ffn(x, mod, per_batch_mod, lw, tm):
    bsz, length, _ = x.shape
    n_blk = length // tm
    per = tm // HALO
    n_halo = length // HALO
    mod_map = (lambda b, i: (b, 0, 0)) if per_batch_mod else (lambda b, i: (0, 0, 0))
    return pl.pallas_call(
        functools.partial(_ffn_kernel, tm, n_blk),
        out_shape=jax.ShapeDtypeStruct((bsz, length, D_MODEL), F32),
        grid=(bsz, n_blk),
        in_specs=[pl.BlockSpec((None, tm, D_MODEL), lambda b, i: (b, i, 0)),
                  pl.BlockSpec((None, HALO, D_MODEL),
                               lambda b, i: (b, jnp.maximum(i * per - 1, 0), 0)),
                  pl.BlockSpec((None, HALO, D_MODEL),
                               lambda b, i: (b, jnp.minimum((i + 1) * per, n_halo - 1), 0)),
                  pl.BlockSpec((None, 1, 6 * D_MODEL), mod_map),
                  _const_spec((1, D_MODEL)), _const_spec((D_MODEL, 2 * D_FF)),
                  _const_spec((SUBLANES, 2 * D_FF)), _const_spec((D_FF, D_MODEL)),
                  _const_spec((1, D_MODEL))],
        out_specs=pl.BlockSpec((None, tm, D_MODEL), lambda b, i: (b, i, 0)),
        scratch_shapes=[pltpu.VMEM((tm + 2 * HALO, D_MODEL), F32)],
        compiler_params=_params("parallel", "parallel"),
        name="ffn",
    )(x, x, x, mod, lw['norm_ffn_pre'], lw['ffn_w_up'], lw['ffn_conv'], lw['ffn_w_down'],
      lw['norm_ffn_post'])


def _pad_rows(w, rows):
    return jnp.pad(w, ((0, rows - w.shape[0]), (0, 0)))


def _lane_row(values, offset):
    return jnp.pad(values.reshape(1, -1), ((0, 0), (offset, LANES - offset - values.size)))


def _layer_weights(l, p):
    o = IN_OFFSETS
    w_in = p['w_in'][l]
    col = lambda k: w_in[:, o[k]:o[k] + IN_SIZES[k]]
    small = jnp.concatenate([col(6), col(2), col(3), col(9)], axis=1)
    small = jnp.pad(small, ((0, 0), (0, LANES - small.shape[1])))
    w_in_p = jnp.concatenate([col(0), col(1), col(4), col(5), col(7), col(8), small], axis=1)

    head_pad = HEAD_BLOCK - MLA_NOPE - MLA_ROPE
    w_uq = p['mla_w_uq'][l].reshape(MLA_Q_LORA, MLA_HEADS, MLA_NOPE + MLA_ROPE)
    w_uq = jnp.pad(w_uq, ((0, 0), (0, 0), (0, head_pad))).reshape(MLA_Q_LORA, -1)
    w_ukv = p['mla_w_ukv'][l].reshape(MLA_KV_LORA, MLA_HEADS, MLA_NOPE + MLA_V)
    w_k = jnp.pad(w_ukv[:, :, :MLA_NOPE], ((0, 0), (0, 0), (0, HEAD_BLOCK - MLA_NOPE)))
    w_k = w_k.reshape(MLA_KV_LORA, -1)
    w_v = w_ukv[:, :, MLA_NOPE:].reshape(MLA_KV_LORA, -1)
    place = jnp.pad(jnp.eye(MLA_ROPE, dtype=F32), ((0, LANES - MLA_ROPE), (MLA_NOPE, head_pad)))
    p_k = jnp.tile(place, (1, MLA_HEADS))

    gdn_prow = jnp.concatenate([_lane_row(p['gdn_a_log'][l], SM_GA),
                                _lane_row(p['gdn_dt_bias'][l], SM_GA)], axis=0)
    ssd_prow = jnp.concatenate([_lane_row(p['ssd_a_log'][l], SM_DT),
                                _lane_row(p['ssd_dt_bias'][l], SM_DT)], axis=0)
    row = lambda v: v.reshape(1, -1)
    return {
        'norm_mix_pre': row(p['norm_mix_pre'][l]), 'norm_mix_post': row(p['norm_mix_post'][l]),
        'norm_ffn_pre': row(p['norm_ffn_pre'][l]), 'norm_ffn_post': row(p['norm_ffn_post'][l]),
        'w_in': w_in_p.astype(BF16),
        'mla_q_norm': row(p['mla_q_norm'][l]), 'w_uq': w_uq.astype(BF16),
        'mla_kv_norm': row(p['mla_kv_norm'][l]), 'w_k': w_k.astype(BF16),
        'w_vt': w_v.T.astype(BF16), 'p_k': p_k.astype(BF16),
        'gdn_conv': _pad_rows(p['gdn_conv'][l], SUBLANES), 'gdn_prow': _pad_rows(gdn_prow, SUBLANES),
        'gdn_norm': row(jnp.tile(p['gdn_norm'][l], GDN_HEADS)),
        'ssd_conv': _pad_rows(p['ssd_conv'][l], SUBLANES), 'ssd_conv_b': row(p['ssd_conv_b'][l]),
        'ssd_prow': _pad_rows(ssd_prow, SUBLANES),
        'ssd_d': row(jnp.repeat(p['ssd_d'][l], SSD_HEAD_DIM)),
        'ssd_norm': row(p['ssd_norm'][l]),
        'w_out': p['w_out'][l].astype(BF16),
        'ffn_w_up': p['ffn_w_up'][l].astype(BF16),
        'ffn_conv': _pad_rows(p['ffn_conv'][l], SUBLANES),
        'ffn_w_down': p['ffn_w_down'][l].astype(BF16),
    }


def _rope_table(n_tokens):
    f32 = np.float32
    t = np.arange(n_tokens, dtype=np.int32)
    r = (t // GRID_W).astype(f32)
    col = (t % GRID_W).astype(f32)
    nf = MLA_ROPE // 4
    half = MLA_ROPE // 2
    inv = np.power(f32(ROPE_THETA), -np.arange(nf, dtype=f32) / f32(nf)).astype(f32)
    ang = np.concatenate([r[:, None] * inv, col[:, None] * inv], axis=-1).astype(f32)
    cos, sin = np.cos(ang).astype(f32), np.sin(ang).astype(f32)
    zeros = np.zeros_like(sin)

    def lanes(first, x1, x2, fill):
        out = np.full((n_tokens, LANES), fill, f32)
        out[:, first:first + half] = x1
        out[:, first + half:first + 2 * half] = x2
        return out

    parts = []
    for first in (MLA_NOPE, SM_KR):
        parts += [lanes(first, cos, cos, 1.0), lanes(first, zeros, sin, 0.0),
                  lanes(first, -sin, zeros, 0.0)]
    return jnp.asarray(np.concatenate(parts, axis=-1))


def _trunk_layer(x, mod, per_batch_mod, lw, rope_tab, ctx, tm, tq):
    bsz, length, _ = x.shape
    tb = min(SCAN_BLOCK, length)
    (qkv_pre, gate, qm, ckv, kmat, vmat, z, xbc_pre, small) = _inproj(
        x, mod, per_batch_mod, lw, rope_tab, tm)
    if ctx is None:
        kmats, vmats = [kmat], [vmat]
        s_gdn = jnp.zeros((2, bsz, GDN_HEADS, GDN_DK, GDN_DV), F32)
        s_ssd = jnp.zeros((2, bsz, SSD_HEADS, SSD_HEAD_DIM, SSD_STATE), F32)
    else:
        ctx_ckv, ctx_kr, st_gdn, st_ssd = ctx
        kr_pad = jnp.pad(ctx_kr, ((0, 0), (0, 0), (SM_KR, LANES - SM_KR - MLA_ROPE)))
        k_ctx, v_ctx = _ctxkv(ctx_ckv, kr_pad, lw)
        kmats, vmats = [kmat, k_ctx], [vmat, v_ctx]
        s_gdn = jnp.moveaxis(st_gdn, 1, 0)
        s_ssd = jnp.moveaxis(st_ssd, 1, 0)
    n_blk = length // tb
    (o_f, sg_f, qkv), (y_f, ss_f, xbc) = _scan(
        _gdn(qkv_pre, small, lw['gdn_conv'], lw['gdn_prow'], s_gdn[0], False, tb),
        _ssd(xbc_pre, small, lw['ssd_conv'], lw['ssd_conv_b'], lw['ssd_prow'], lw['ssd_d'],
             s_ssd[0], False, tb),
        bsz, n_blk, False, tb)
    (o_b, sg_b), (y_b, ss_b) = _scan(
        _gdn(qkv, small, None, lw['gdn_prow'], s_gdn[1], True, tb),
        _ssd(xbc, small, None, None, lw['ssd_prow'], None, s_ssd[1], True, tb),
        bsz, n_blk, True, tb)
    o_mla = _attention(qm, kmats, vmats, tq)
    x = _outproj(x, mod, per_batch_mod, o_f, o_b, gate, o_mla, y_f, y_b, z, lw, tm)
    x = _ffn(x, mod, per_batch_mod, lw, min(FFN_ROWS, length))
    ctx_out = (ckv, small[:, :, SM_KR:SM_KR + MLA_ROPE], jnp.stack([sg_f, sg_b], axis=1),
               jnp.stack([ss_f, ss_b], axis=1))
    return x, ctx_out


def kernel(x_prompt, x_sample, cache_mla_ckv, cache_mla_krope, state_gdn, state_ssd, c, c_ctx, w_ada, b_ada, norm_mix_pre, norm_mix_post, norm_ffn_pre, norm_ffn_post, w_in, gdn_conv, gdn_a_log, gdn_dt_bias, gdn_norm, mla_q_norm, mla_w_uq, mla_kv_norm, mla_w_ukv, ssd_conv, ssd_conv_b, ssd_a_log, ssd_dt_bias, ssd_d, ssd_norm, w_out, ffn_w_up, ffn_conv, ffn_w_down):
    p = dict(norm_mix_pre=norm_mix_pre, norm_mix_post=norm_mix_post, norm_ffn_pre=norm_ffn_pre,
             norm_ffn_post=norm_ffn_post, w_in=w_in, gdn_conv=gdn_conv, gdn_a_log=gdn_a_log,
             gdn_dt_bias=gdn_dt_bias, gdn_norm=gdn_norm, mla_q_norm=mla_q_norm,
             mla_w_uq=mla_w_uq, mla_kv_norm=mla_kv_norm, mla_w_ukv=mla_w_ukv, ssd_conv=ssd_conv,
             ssd_conv_b=ssd_conv_b, ssd_a_log=ssd_a_log, ssd_dt_bias=ssd_dt_bias, ssd_d=ssd_d,
             ssd_norm=ssd_norm, w_out=w_out, ffn_w_up=ffn_w_up, ffn_conv=ffn_conv,
             ffn_w_down=ffn_w_down)
    dec_batch = x_sample.shape[0]
    cond8 = jnp.concatenate(
        [c, c_ctx[None, :], jnp.zeros((SUBLANES - dec_batch - 1, D_MODEL), F32)], axis=0)
    mod = _ada(cond8, w_ada, b_ada)
    rope_tab = _rope_table(x_sample.shape[1])

    y_prompt, y_sample = x_prompt, x_sample
    ckv_l, kr_l, sg_l, ss_l = [], [], [], []
    for l in range(DEPTH):
        lw = _layer_weights(l, p)
        mod_s = mod[l, 0:dec_batch].reshape(dec_batch, 1, 6 * D_MODEL)
        mod_p = mod[l, dec_batch:dec_batch + 1].reshape(1, 1, 6 * D_MODEL)
        y_prompt, (ckv, kr, sg, ss) = _trunk_layer(
            y_prompt, mod_p, False, lw, None, None, tm=256, tq=256)
        ckv_l.append(ckv)
        kr_l.append(kr)
        sg_l.append(sg)
        ss_l.append(ss)
        y_sample, _ = _trunk_layer(
            y_sample, mod_s, True, lw, rope_tab,
            (cache_mla_ckv[:, l], cache_mla_krope[:, l], state_gdn[:, l], state_ssd[:, l]),
            tm=512, tq=256)
    return (y_prompt, y_sample, jnp.stack(ckv_l, axis=1), jnp.stack(kr_l, axis=1),
            jnp.stack(sg_l, axis=1), jnp.stack(ss_l, axis=1))
```

```python
import functools
import itertools

import numpy as np
import jax
import jax.numpy as jnp
from jax import lax
from jax.experimental import pallas as pl
from jax.experimental.pallas import tpu as pltpu

F32 = jnp.float32
BF16 = jnp.bfloat16

D_MODEL = 1024
DEPTH = 2
GRID_W = 64
ROPE_THETA = 10000.0
EPS = 1e-6
LOG2_E = 1.4426950408889634
CHUNK = 64
CONV_K = 5
FFN_CONV_K = 3

GDN_HEADS = 4
GDN_DK = 64
GDN_DV = 64
GDN_WIDTH = GDN_HEADS * GDN_DV

MLA_HEADS = 8
MLA_Q_LORA = 256
MLA_KV_LORA = 128
MLA_NOPE = 64
MLA_ROPE = 32
MLA_V = 64
MLA_WIDTH = MLA_HEADS * MLA_V

SSD_HEADS = 4
SSD_HEAD_DIM = 64
SSD_INNER = SSD_HEADS * SSD_HEAD_DIM
SSD_GROUPS = 2
SSD_STATE = 128
SSD_CONV_CH = SSD_INNER + 2 * SSD_GROUPS * SSD_STATE

D_FF = 128 * ((8 * D_MODEL // 3 + 127) // 128)

IN_SIZES = (3 * GDN_WIDTH, GDN_WIDTH, 2 * GDN_HEADS, 2 * GDN_HEADS,
            MLA_Q_LORA, MLA_KV_LORA, MLA_ROPE,
            SSD_INNER, SSD_CONV_CH, 2 * SSD_HEADS)
IN_OFFSETS = tuple(int(o) for o in np.cumsum((0,) + IN_SIZES)[:-1])

LANES = 128
SUBLANES = 8
VMEM_LIMIT_BYTES = 56 * 1024 * 1024

SM_KR = 0
SM_GA = MLA_ROPE
SM_GB = SM_GA + 2 * GDN_HEADS
SM_DT = SM_GB + 2 * GDN_HEADS
PC_QKV = 0
PC_GATE = PC_QKV + 3 * GDN_WIDTH
PC_CQ = PC_GATE + GDN_WIDTH
PC_CKV = PC_CQ + MLA_Q_LORA
PC_Z = PC_CKV + MLA_KV_LORA
PC_XBC = PC_Z + SSD_INNER
PC_SMALL = PC_XBC + SSD_CONV_CH
PC_TOTAL = PC_SMALL + LANES
HEAD_BLOCK = LANES
SCAN_BLOCK = 512
HALO = SUBLANES


def _bdot(a, b):
    return jnp.dot(a.astype(BF16), b.astype(BF16), preferred_element_type=F32)


def _bdot_nt(a, b):
    return lax.dot_general(a.astype(BF16), b.astype(BF16), (((1,), (1,)), ((), ())),
                           preferred_element_type=F32)


def _bdot_tn(a, b):
    return lax.dot_general(a.astype(BF16), b.astype(BF16), (((0,), (0,)), ((), ())),
                           preferred_element_type=F32)


def _fdot(a, b):
    return jnp.dot(a, b, precision=lax.Precision.HIGHEST, preferred_element_type=F32)


def _silu(x):
    return x * (1.0 / (1.0 + jnp.exp(-x)))


def _sigmoid(x):
    return 1.0 / (1.0 + jnp.exp(-x))


def _softplus(x):
    return jnp.maximum(x, 0.0) + jnp.log(1.0 + jnp.exp(-jnp.abs(x)))


def _rms(x, gain):
    ms = jnp.mean(x * x, axis=-1, keepdims=True)
    return x * lax.rsqrt(ms + EPS) * gain


def _params(*sem):
    return pltpu.CompilerParams(dimension_semantics=sem, vmem_limit_bytes=VMEM_LIMIT_BYTES)


def _const_spec(shape):
    nd = len(shape)
    return pl.BlockSpec(shape, lambda *_: (0,) * nd, pipeline_mode=pl.Buffered(1))


def _ada_kernel(cond_ref, w_ref, b_ref, o_ref):
    a = _silu(cond_ref[...])
    o_ref[...] = jnp.dot(a, w_ref[...], precision=lax.Precision.HIGHEST,
                         preferred_element_type=F32) + b_ref[...]


def _ada(cond8, w_ada, b_ada):
    n_col = 6 * D_MODEL // D_MODEL
    return pl.pallas_call(
        _ada_kernel,
        out_shape=jax.ShapeDtypeStruct((DEPTH, SUBLANES, 6 * D_MODEL), F32),
        grid=(DEPTH, n_col),
        in_specs=[pl.BlockSpec((SUBLANES, D_MODEL), lambda l, j: (0, 0)),
                  pl.BlockSpec((None, D_MODEL, D_MODEL), lambda l, j: (l, 0, j)),
                  pl.BlockSpec((None, 1, D_MODEL), lambda l, j: (l, 0, j))],
        out_specs=pl.BlockSpec((None, SUBLANES, D_MODEL), lambda l, j: (l, 0, j)),
        compiler_params=_params("arbitrary", "arbitrary"),
        name="ada",
    )(cond8, w_ada, b_ada.reshape(DEPTH, 1, 6 * D_MODEL))


def _rope_lanes(x, c, s_up, s_dn):
    half = MLA_ROPE // 2
    return (x * c + pltpu.roll(x, half, 1) * s_up + pltpu.roll(x, LANES - half, 1) * s_dn)


def _inproj_kernel(use_rope, *refs):
    if use_rope:
        (x_ref, mod_ref, gpre_ref, win_ref, qn_ref, wuq_ref, kvn_ref, wk_ref, wv_ref, pk_ref,
         rope_ref, qkv_o, gate_o, qm_o, ckv_o, kmat_o, vmat_o, z_o, xbc_o, small_o) = refs
    else:
        (x_ref, mod_ref, gpre_ref, win_ref, qn_ref, wuq_ref, kvn_ref, wk_ref, wv_ref, pk_ref,
         qkv_o, gate_o, qm_o, ckv_o, kmat_o, vmat_o, z_o, xbc_o, small_o) = refs
    x = x_ref[...]
    sh1 = mod_ref[:, 0:D_MODEL]
    sc1 = mod_ref[:, D_MODEL:2 * D_MODEL]
    h = _rms(x, gpre_ref[...]) * (1.0 + sc1) + sh1
    proj = _bdot(h, win_ref[...])
    qkv_o[...] = proj[:, PC_QKV:PC_GATE]
    gate_o[...] = proj[:, PC_GATE:PC_CQ]
    z_o[...] = proj[:, PC_Z:PC_XBC]
    xbc_o[...] = proj[:, PC_XBC:PC_SMALL]
    small = proj[:, PC_SMALL:PC_TOTAL]
    small_o[...] = small

    cq = _rms(proj[:, PC_CQ:PC_CKV], qn_ref[...])
    qm = _bdot(cq, wuq_ref[...]) * ((MLA_NOPE + MLA_ROPE) ** -0.5 * LOG2_E)
    ckv = _rms(proj[:, PC_CKV:PC_Z], kvn_ref[...])
    ckv_o[...] = ckv
    if use_rope:
        cq_t = rope_ref[:, 0:LANES]
        s1q_t = rope_ref[:, LANES:2 * LANES]
        s2q_t = rope_ref[:, 2 * LANES:3 * LANES]
        for hd in range(MLA_HEADS):
            blk = qm[:, hd * HEAD_BLOCK:(hd + 1) * HEAD_BLOCK]
            qm_o[:, hd * HEAD_BLOCK:(hd + 1) * HEAD_BLOCK] = _rope_lanes(
                blk, cq_t, s1q_t, s2q_t).astype(BF16)
        kr = _rope_lanes(small, rope_ref[:, 3 * LANES:4 * LANES],
                         rope_ref[:, 4 * LANES:5 * LANES], rope_ref[:, 5 * LANES:6 * LANES])
    else:
        qm_o[...] = qm.astype(BF16)
        kr = small
    kmat_o[...] = (_bdot(ckv, wk_ref[...]) + _bdot(kr, pk_ref[...])).astype(BF16)
    vmat_o[...] = _bdot_nt(wv_ref[...], ckv).astype(BF16)


def _inproj(x, mod, per_batch_mod, lw, rope_tab, tm):
    bsz, length, _ = x.shape
    use_rope = rope_tab is not None
    nb = length // tm
    tok = lambda w: pl.BlockSpec((None, tm, w), lambda b, i: (b, i, 0))
    mod_map = (lambda b, i: (b, 0, 0)) if per_batch_mod else (lambda b, i: (0, 0, 0))
    in_specs = [tok(D_MODEL),
                pl.BlockSpec((None, 1, 6 * D_MODEL), mod_map),
                _const_spec((1, D_MODEL)),
                _const_spec((D_MODEL, PC_TOTAL)),
                _const_spec((1, MLA_Q_LORA)),
                _const_spec((MLA_Q_LORA, MLA_HEADS * HEAD_BLOCK)),
                _const_spec((1, MLA_KV_LORA)),
                _const_spec((MLA_KV_LORA, MLA_HEADS * HEAD_BLOCK)),
                _const_spec((MLA_WIDTH, MLA_KV_LORA)),
                _const_spec((LANES, MLA_HEADS * HEAD_BLOCK))]
    args = [x, mod, lw['norm_mix_pre'], lw['w_in'], lw['mla_q_norm'], lw['w_uq'],
            lw['mla_kv_norm'], lw['w_k'], lw['w_vt'], lw['p_k']]
    if use_rope:
        in_specs.append(pl.BlockSpec((tm, 6 * LANES), lambda b, i: (i, 0)))
        args.append(rope_tab)
    widths = [(3 * GDN_WIDTH, F32), (GDN_WIDTH, F32), (MLA_HEADS * HEAD_BLOCK, BF16),
              (MLA_KV_LORA, F32), (MLA_HEADS * HEAD_BLOCK, BF16), None,
              (SSD_INNER, F32), (SSD_CONV_CH, F32), (LANES, F32)]
    out_shape = [jax.ShapeDtypeStruct((bsz, MLA_WIDTH, length), BF16) if w is None
                 else jax.ShapeDtypeStruct((bsz, length, w[0]), w[1]) for w in widths]
    out_specs = [pl.BlockSpec((None, MLA_WIDTH, tm), lambda b, i: (b, 0, i)) if w is None
                 else tok(w[0]) for w in widths]
    return pl.pallas_call(
        functools.partial(_inproj_kernel, use_rope),
        out_shape=out_shape,
        grid=(bsz, nb),
        in_specs=in_specs,
        out_specs=out_specs,
        compiler_params=_params("parallel", "parallel"),
        name="inproj",
    )(*args)


def _ctxkv_kernel(ckv_ref, kr_ref, wk_ref, wv_ref, pk_ref, kmat_o, vmat_o):
    ckv = ckv_ref[...]
    kmat_o[...] = (_bdot(ckv, wk_ref[...]) + _bdot(kr_ref[...], pk_ref[...])).astype(BF16)
    vmat_o[...] = _bdot_nt(wv_ref[...], ckv).astype(BF16)


def _ctxkv(ckv, kr_pad, lw):
    bsz, length, _ = ckv.shape
    tok = lambda w: pl.BlockSpec((None, length, w), lambda b: (b, 0, 0))
    return pl.pallas_call(
        _ctxkv_kernel,
        out_shape=[jax.ShapeDtypeStruct((bsz, length, MLA_HEADS * HEAD_BLOCK), BF16),
                   jax.ShapeDtypeStruct((bsz, MLA_WIDTH, length), BF16)],
        grid=(bsz,),
        in_specs=[tok(MLA_KV_LORA), tok(LANES),
                  _const_spec((MLA_KV_LORA, MLA_HEADS * HEAD_BLOCK)),
                  _const_spec((MLA_WIDTH, MLA_KV_LORA)),
                  _const_spec((LANES, MLA_HEADS * HEAD_BLOCK))],
        out_specs=[tok(MLA_HEADS * HEAD_BLOCK),
                   pl.BlockSpec((None, MLA_WIDTH, length), lambda b: (b, 0, 0))],
        compiler_params=_params("parallel"),
        name="ctxkv",
    )(ckv, kr_pad, lw['w_k'], lw['w_vt'], lw['p_k'])


ATTN_HEADS = 8
ATTN_KEY_CHUNK = 1024


def _attn_kernel(n_seg, *refs):
    q_ref = refs[0]
    k_refs = refs[1:1 + n_seg]
    v_refs = refs[1 + n_seg:1 + 2 * n_seg]
    o_ref = refs[1 + 2 * n_seg]
    chunks = []
    for seg, k in enumerate(k_refs):
        for r0 in range(0, k.shape[0], ATTN_KEY_CHUNK):
            chunks.append((seg, r0, min(ATTN_KEY_CHUNK, k.shape[0] - r0)))

    def scores(hh, chunk):
        seg, r0, n = chunk
        s = lax.dot_general(k_refs[seg][r0:r0 + n, hh * HEAD_BLOCK:(hh + 1) * HEAD_BLOCK],
                            q_ref[:, hh * HEAD_BLOCK:(hh + 1) * HEAD_BLOCK],
                            (((1,), (1,)), ((), ())), preferred_element_type=F32)
        return s, s.max(axis=0, keepdims=True)

    def weigh(hh, chunk, s, m):
        seg, r0, n = chunk
        p = jnp.exp2(s - m)
        pv = jnp.dot(v_refs[seg][hh * MLA_V:(hh + 1) * MLA_V, r0:r0 + n], p.astype(BF16),
                     preferred_element_type=F32)
        return pv, p.sum(axis=0, keepdims=True)

    outs = []
    nxt = [scores(0, c) for c in chunks]
    for hh in range(ATTN_HEADS):
        cur = nxt
        nxt = []
        m = cur[0][1]
        for _, mc in cur[1:]:
            m = jnp.maximum(m, mc)
        acc = None
        den = None
        for ci, c in enumerate(chunks):
            if hh + 1 < ATTN_HEADS:
                nxt.append(scores(hh + 1, c))
            pv, d = weigh(hh, c, cur[ci][0], m)
            acc = pv if acc is None else acc + pv
            den = d if den is None else den + d
        outs.append(acc / den)
    o_ref[...] = jnp.concatenate(outs, axis=0).T.astype(BF16)


def _attention(qm, kmats, vmats, tq):
    bsz, length, _ = qm.shape
    n_seg = len(kmats)
    n_grp = MLA_HEADS // ATTN_HEADS
    in_specs = [pl.BlockSpec((None, tq, ATTN_HEADS * HEAD_BLOCK), lambda b, p, i: (b, i, p))]
    for k in kmats:
        in_specs.append(pl.BlockSpec((None, k.shape[1], ATTN_HEADS * HEAD_BLOCK),
                                     lambda b, p, i: (b, 0, p)))
    for v in vmats:
        in_specs.append(pl.BlockSpec((None, ATTN_HEADS * MLA_V, v.shape[2]),
                                     lambda b, p, i: (b, p, 0)))
    return pl.pallas_call(
        functools.partial(_attn_kernel, n_seg),
        out_shape=jax.ShapeDtypeStruct((bsz, length, MLA_WIDTH), BF16),
        grid=(bsz, n_grp, length // tq),
        in_specs=in_specs,
        out_specs=pl.BlockSpec((None, tq, ATTN_HEADS * MLA_V), lambda b, p, i: (b, i, p)),
        compiler_params=_params("parallel", "parallel", "arbitrary"),
        name="attn",
    )(qm, *kmats, *vmats)


def _conv_tile(xe_ref, prev_ref, main_ref, next_ref, w_ref, blk, n_blk, tb):
    xe_ref[0:HALO, :] = jnp.where(blk > 0, prev_ref[...], 0.0)
    xe_ref[HALO:HALO + tb, :] = main_ref[...]
    xe_ref[HALO + tb:HALO + tb + HALO, :] = jnp.where(blk < n_blk - 1, next_ref[...], 0.0)
    xe = xe_ref[...]
    rows = tb + 2 * HALO
    acc = None
    for j in range(CONV_K):
        shift = (CONV_K // 2 - j) % rows
        tap = xe if shift == 0 else pltpu.roll(xe, shift, 0)
        term = tap * w_ref[j:j + 1, :]
        acc = term if acc is None else acc + term
    return acc[HALO:HALO + tb, :]


def _chunk_tri(tb, reverse):
    r = lax.broadcasted_iota(jnp.int32, (tb, tb), 0)
    c = lax.broadcasted_iota(jnp.int32, (tb, tb), 1)
    same = (r // CHUNK) == (c // CHUNK)
    order = (c >= r) if reverse else (c <= r)
    return jnp.where(same & order, 1.0, 0.0).astype(BF16)


def _masks(reverse, width):
    r = lax.broadcasted_iota(jnp.int32, (CHUNK, width), 0)
    c = lax.broadcasted_iota(jnp.int32, (CHUNK, width), 1) % CHUNK
    eye = jnp.where(c == r, 1.0, 0.0).astype(F32)
    if reverse:
        return c >= r, c > r, eye
    return c <= r, c < r, eye


def _lane_block(shape, per):
    return lax.broadcasted_iota(jnp.int32, shape, 1) // per


def _bd(x, n_blk):
    blk = _lane_block(x.shape, x.shape[1] // n_blk)
    return jnp.concatenate([jnp.where(blk == h, x, 0.0) for h in range(n_blk)], axis=0)


def _split2(x):
    hi = x.astype(BF16).astype(F32)
    return hi, x - hi


def _split3(x):
    hi, r1 = _split2(x)
    mid, lo = _split2(r1)
    return hi.astype(BF16), mid.astype(BF16), lo.astype(BF16)


def _sel_dot(x, sel):
    return jnp.dot(jnp.concatenate(_split3(x), axis=1), jnp.concatenate([sel, sel, sel], axis=0),
                   preferred_element_type=F32)


def _tri_dot(tri, x):
    return jnp.dot(jnp.concatenate([tri, tri, tri], axis=1), jnp.concatenate(_split3(x), axis=0),
                   preferred_element_type=F32)


def _lane_select(rows, first, width, per):
    r = lax.broadcasted_iota(jnp.int32, (rows, width), 0)
    c = lax.broadcasted_iota(jnp.int32, (rows, width), 1)
    return jnp.where(r == first + c // per, 1.0, 0.0).astype(BF16)


def _scan_specs(bsz, length, width, tb, reverse):
    n_blk = length // tb
    per = tb // HALO
    n_halo = length // HALO
    blk_of = (lambda i: n_blk - 1 - i) if reverse else (lambda i: i)
    main = pl.BlockSpec((None, tb, width), lambda b, i: (b, blk_of(i), 0))
    prev = pl.BlockSpec((None, HALO, width),
                        lambda b, i: (b, jnp.maximum(blk_of(i) * per - 1, 0), 0))
    nxt = pl.BlockSpec((None, HALO, width),
                       lambda b, i: (b, jnp.minimum((blk_of(i) + 1) * per, n_halo - 1), 0))
    small = pl.BlockSpec((None, tb, LANES), lambda b, i: (b, blk_of(i), 0))
    return n_blk, blk_of, main, prev, nxt, small


def _gdn_body(reverse, tb, n_blk, refs):
    if reverse:
        qkv_s, small_ref, prow_ref, s0_ref, o_ref, sfin_ref, g_s, b_s, state = refs
    else:
        (main_ref, prev_ref, next_ref, small_ref, w_ref, prow_ref, s0_ref,
         o_ref, sfin_ref, qkv_s, xe_ref, g_s, b_s, state) = refs
    i = pl.program_id(1)
    d = 1 if reverse else 0

    @pl.when(i == 0)
    def _():
        state[...] = s0_ref[...]

    nh = GDN_HEADS
    if not reverse:
        qkv = _silu(_conv_tile(xe_ref, prev_ref, main_ref, next_ref, w_ref, i, n_blk, tb))
        q = qkv[:, 0:GDN_WIDTH]
        k = qkv[:, GDN_WIDTH:2 * GDN_WIDTH]
        head_ones = jnp.where(
            lax.broadcasted_iota(jnp.int32, (GDN_WIDTH, GDN_WIDTH), 0) // GDN_DK
            == _lane_block((GDN_WIDTH, GDN_WIDTH), GDN_DK), 1.0, 0.0).astype(BF16)
        qkv_s[:, 0:GDN_WIDTH] = q * lax.rsqrt(_sel_dot(q * q, head_ones) + EPS) * (GDN_DK ** -0.5)
        qkv_s[:, GDN_WIDTH:2 * GDN_WIDTH] = k * lax.rsqrt(_sel_dot(k * k, head_ones) + EPS)
        qkv_s[:, 2 * GDN_WIDTH:3 * GDN_WIDTH] = qkv[:, 2 * GDN_WIDTH:3 * GDN_WIDTH]

    yield
    small = small_ref[...]
    log_a = -jnp.exp(prow_ref[0:1, :]) * _softplus(small + prow_ref[1:2, :])
    packed = jnp.where(lax.broadcasted_iota(jnp.int32, small.shape, 1) < SM_GB, log_a,
                       _sigmoid(small))
    sel = jnp.concatenate([_lane_select(LANES, SM_GA + d * nh, GDN_WIDTH, GDN_DK),
                           _lane_select(LANES, SM_GB + d * nh, GDN_WIDTH, GDN_DK)], axis=1)
    spread = _sel_dot(packed, sel)
    g_s[...] = _tri_dot(_chunk_tri(tb, reverse), spread[:, 0:GDN_WIDTH])
    b_s[...] = spread[:, GDN_WIDTH:2 * GDN_WIDTH]

    yield
    incl, strict, eye = _masks(reverse, GDN_WIDTH)
    n_chunk = tb // CHUNK
    last = 0 if reverse else CHUNK - 1
    chunks = range(n_chunk)
    row_sl = [slice(c * CHUNK, (c + 1) * CHUNK) for c in chunks]

    g_col = [g_s[r, :] for r in row_sl]
    beta = [b_s[r, :] for r in row_sl]
    g_last = [g[last:last + 1, :] for g in g_col]
    decay = []
    for g in g_col:
        g_row = jnp.sum(eye * g, axis=0, keepdims=True)
        decay.append(jnp.where(incl, jnp.exp(jnp.where(incl, g - g_row, 0.0)), 0.0))
    q4 = [qkv_s[r, 0:GDN_WIDTH] for r in row_sl]
    k4 = [qkv_s[r, GDN_WIDTH:2 * GDN_WIDTH] for r in row_sl]
    qk_kk = [_bdot_nt(jnp.concatenate([q4[c], k4[c]], axis=0), _bd(k4[c], nh)) for c in chunks]
    m_pow = [-jnp.where(strict, qk_kk[c][CHUNK:2 * CHUNK, :] * decay[c], 0.0) * beta[c]
             for c in chunks]
    yield
    t_inv = [eye + m for m in m_pow]
    n_sq = 6
    for step in range(n_sq):
        for c in chunks:
            m_hi, m_lo = _split2(m_pow[c])
            if step == 0:
                lhs_f = [(m_hi, m_lo)]
            elif step == n_sq - 1:
                lhs_f = [_split2(t_inv[c])]
            else:
                lhs_f = [(m_hi, m_lo), _split2(t_inv[c])]
            hi = jnp.concatenate([p[0] for p in lhs_f], axis=0).astype(BF16)
            lo = jnp.concatenate([p[1] for p in lhs_f], axis=0).astype(BF16)
            n_rows = hi.shape[0]
            both = jnp.dot(jnp.concatenate([hi, lo], axis=0), _bd(m_hi, nh).astype(BF16),
                           preferred_element_type=F32)
            prod = (both[0:n_rows, :] + both[n_rows:2 * n_rows, :]
                    + jnp.dot(hi, _bd(m_lo, nh).astype(BF16), preferred_element_type=F32))
            if step == 0:
                m_pow[c] = prod
            elif step == n_sq - 1:
                t_inv[c] = t_inv[c] + prod
            else:
                t_inv[c] = t_inv[c] + prod[CHUNK:2 * CHUNK, :]
                m_pow[c] = prod[0:CHUNK, :]
        yield
    uw, attn, q_dec, k_dec = [], [], [], []
    for c in chunks:
        eg = jnp.exp(g_col[c])
        v4 = qkv_s[row_sl[c], 2 * GDN_WIDTH:3 * GDN_WIDTH]
        uw.append(_bdot(t_inv[c], jnp.concatenate(
            [_bd(v4 * beta[c], nh), _bd(k4[c] * (beta[c] * eg), nh)], axis=1)))
        attn.append(jnp.where(incl, qk_kk[c][0:CHUNK, :] * decay[c], 0.0))
        q_dec.append(q4[c] * eg)
        k_dec.append(k4[c] * jnp.exp(g_last[c] - g_col[c]))

    yield
    blk_id = _lane_block((GDN_DK, GDN_WIDTH), GDN_DV)

    def head_blocks(x):
        out = None
        for hd in range(nh):
            part = jnp.where(blk_id == hd, x[hd * GDN_DK:(hd + 1) * GDN_DK, :], 0.0)
            out = part if out is None else out + part
        return out

    k_u, k_w = [], []
    for c in chunks:
        kuw = _bdot_tn(k_dec[c], uw[c])
        k_u.append(head_blocks(kuw[:, 0:GDN_WIDTH]))
        k_w.append(head_blocks(kuw[:, GDN_WIDTH:2 * GDN_WIDTH]))

    yield

    def emit_output(c, s_in):
        ws_qs = _bdot(jnp.concatenate([uw[c][:, GDN_WIDTH:2 * GDN_WIDTH], q_dec[c]], axis=0),
                      _bd(s_in, nh))
        v_new = uw[c][:, 0:GDN_WIDTH] - ws_qs[0:CHUNK, :]
        o_ref[row_sl[c], :] = ws_qs[CHUNK:2 * CHUNK, :] + _bdot(attn[c], _bd(v_new, nh))

    s4 = state[...]
    prev = None
    for ci in chunks:
        c = (n_chunk - 1 - ci) if reverse else ci
        s_in = s4
        s4 = s4 * jnp.exp(g_last[c]) + k_u[c] - _bdot(k_w[c], _bd(s4, nh))
        if prev is not None:
            emit_output(*prev)
        prev = (c, s_in)
    emit_output(*prev)
    state[...] = s4
    sfin_ref[...] = s4


def _gdn(qkv, small, conv_w, prow, s0, reverse, tb):
    bsz, length, width = qkv.shape
    n_blk, blk_of, main, prev, nxt, sm = _scan_specs(bsz, length, width, tb, reverse)
    st = pl.BlockSpec((None, GDN_DK, GDN_WIDTH), lambda b, i: (b, 0, 0))
    s0 = s0.transpose(0, 2, 1, 3).reshape(bsz, GDN_DK, GDN_WIDTH)
    out_shape = [jax.ShapeDtypeStruct((bsz, length, GDN_WIDTH), F32),
                 jax.ShapeDtypeStruct((bsz, GDN_DK, GDN_WIDTH), F32)]
    out_specs = [pl.BlockSpec((None, tb, GDN_WIDTH), lambda b, i: (b, blk_of(i), 0)), st]
    scratch = [pltpu.VMEM((tb, GDN_WIDTH), F32), pltpu.VMEM((tb, GDN_WIDTH), F32),
               pltpu.VMEM((GDN_DK, GDN_WIDTH), F32)]
    if reverse:
        in_specs = [main, sm, _const_spec((SUBLANES, LANES)), st]
        args = (qkv, small, prow, s0)
    else:
        in_specs = [main, prev, nxt, sm, _const_spec((SUBLANES, width)),
                    _const_spec((SUBLANES, LANES)), st]
        args = (qkv, qkv, qkv, small, conv_w, prow, s0)
        out_shape.append(jax.ShapeDtypeStruct((bsz, length, width), F32))
        out_specs.append(main)
        scratch = [pltpu.VMEM((tb + 2 * HALO, width), F32)] + scratch

    def finish(res):
        s_fin = res[1].reshape(bsz, GDN_DK, GDN_HEADS, GDN_DV).transpose(0, 2, 1, 3)
        return (res[0], s_fin) + tuple(res[2:])

    return dict(body=_gdn_body, in_specs=in_specs, args=args, out_shape=out_shape,
                out_specs=out_specs, scratch=scratch, finish=finish)


def _ssd_body(reverse, tb, n_blk, refs):
    if reverse:
        xbc_s, small_ref, prow_ref, s0_ref, y_ref, sfin_ref, dt_s, a_s, state = refs
    else:
        (main_ref, prev_ref, next_ref, small_ref, w_ref, cb_ref, prow_ref, dvec_ref, s0_ref,
         y_ref, sfin_ref, xbc_s, xe_ref, dt_s, a_s, state) = refs
    i = pl.program_id(1)
    d = 1 if reverse else 0

    @pl.when(i == 0)
    def _():
        state[...] = s0_ref[...]

    if not reverse:
        xbc_s[...] = _silu(_conv_tile(xe_ref, prev_ref, main_ref, next_ref, w_ref, i, n_blk, tb)
                           + cb_ref[...])
    b_lo = SSD_INNER
    c_lo = SSD_INNER + SSD_GROUPS * SSD_STATE
    yield
    dt = _softplus(small_ref[...] + prow_ref[1:2, :])
    sel = _lane_select(LANES, SM_DT + d * SSD_HEADS, SSD_INNER, SSD_HEAD_DIM)
    spread = _sel_dot(jnp.concatenate([dt * (-jnp.exp(prow_ref[0:1, :])), dt], axis=0), sel)
    a_s[...] = _tri_dot(_chunk_tri(tb, reverse), spread[0:tb, :])
    dt_s[...] = spread[tb:2 * tb, :]

    yield
    incl, _, eye = _masks(reverse, SSD_INNER)
    n_chunk = tb // CHUNK
    rep = SSD_HEADS // SSD_GROUPS
    last = 0 if reverse else CHUNK - 1
    grp_of_lane = _lane_block((CHUNK, SSD_GROUPS * SSD_STATE), SSD_STATE)
    grp_of_lane_s = _lane_block((SSD_STATE, SSD_INNER), rep * SSD_HEAD_DIM)
    for ci in range(n_chunk):
        cc = (n_chunk - 1 - ci) if reverse else ci
        rows = slice(cc * CHUNK, (cc + 1) * CHUNK)
        a_col = a_s[rows, :]
        a_row = jnp.sum(eye * a_col, axis=0, keepdims=True)
        a_last = a_col[last:last + 1, :]
        lmat = jnp.where(incl, jnp.exp(jnp.where(incl, a_col - a_row, 0.0)), 0.0)
        xs = xbc_s[rows, 0:b_lo]
        bm = xbc_s[rows, b_lo:c_lo]
        cm = xbc_s[rows, c_lo:SSD_CONV_CH]
        xdt = xs * dt_s[rows, :]
        b_heads = jnp.concatenate(
            [jnp.where(grp_of_lane == hd // rep, bm, 0.0) for hd in range(SSD_HEADS)], axis=0)
        scores = _bdot_nt(cm, b_heads) * lmat
        st = state[...]
        s_groups = jnp.concatenate(
            [jnp.where(grp_of_lane_s == g, st, 0.0) for g in range(SSD_GROUPS)], axis=0)
        y = _bdot(scores, _bd(xdt, SSD_HEADS)) + _bdot(cm, s_groups) * jnp.exp(a_col)
        if not reverse:
            y = y + xs * dvec_ref[...]
        y_ref[rows, :] = y
        cs = _bdot_tn(bm, xdt * jnp.exp(a_last - a_col))
        upd = jnp.where(grp_of_lane_s == 0, cs[0:SSD_STATE, :], cs[SSD_STATE:2 * SSD_STATE, :])
        state[...] = st * jnp.exp(a_last) + upd
        yield
    sfin_ref[...] = state[...]


def _ssd(xbc, small, conv_w, conv_b, prow, dvec, s0, reverse, tb):
    bsz, length, width = xbc.shape
    n_blk, blk_of, main, prev, nxt, sm = _scan_specs(bsz, length, width, tb, reverse)
    st = pl.BlockSpec((None, SSD_STATE, SSD_INNER), lambda b, i: (b, 0, 0))
    s0 = s0.transpose(0, 3, 1, 2).reshape(bsz, SSD_STATE, SSD_INNER)
    out_shape = [jax.ShapeDtypeStruct((bsz, length, SSD_INNER), F32),
                 jax.ShapeDtypeStruct((bsz, SSD_STATE, SSD_INNER), F32)]
    out_specs = [pl.BlockSpec((None, tb, SSD_INNER), lambda b, i: (b, blk_of(i), 0)), st]
    scratch = [pltpu.VMEM((tb, SSD_INNER), F32), pltpu.VMEM((tb, SSD_INNER), F32),
               pltpu.VMEM((SSD_STATE, SSD_INNER), F32)]
    if reverse:
        in_specs = [main, sm, _const_spec((SUBLANES, LANES)), st]
        args = (xbc, small, prow, s0)
    else:
        in_specs = [main, prev, nxt, sm, _const_spec((SUBLANES, width)), _const_spec((1, width)),
                    _const_spec((SUBLANES, LANES)), _const_spec((1, SSD_INNER)), st]
        args = (xbc, xbc, xbc, small, conv_w, conv_b, prow, dvec, s0)
        out_shape.append(jax.ShapeDtypeStruct((bsz, length, width), F32))
        out_specs.append(main)
        scratch = [pltpu.VMEM((tb + 2 * HALO, width), F32)] + scratch

    def finish(res):
        s_fin = res[1].reshape(bsz, SSD_STATE, SSD_HEADS, SSD_HEAD_DIM).transpose(0, 2, 3, 1)
        return (res[0], s_fin) + tuple(res[2:])

    return dict(body=_ssd_body, in_specs=in_specs, args=args, out_shape=out_shape,
                out_specs=out_specs, scratch=scratch, finish=finish)


def _scan_kernel(reverse, tb, n_blk, counts, *refs):
    n_in, n_out, n_scr = counts
    ins = refs[:sum(n_in)]
    outs = refs[sum(n_in):sum(n_in) + sum(n_out)]
    scr = refs[sum(n_in) + sum(n_out):]
    gdn_refs = ins[:n_in[0]] + outs[:n_out[0]] + scr[:n_scr[0]]
    ssd_refs = ins[n_in[0]:] + outs[n_out[0]:] + scr[n_scr[0]:]
    for _ in itertools.zip_longest(_gdn_body(reverse, tb, n_blk, gdn_refs),
                                   _ssd_body(reverse, tb, n_blk, ssd_refs)):
        pass


def _scan(gdn, ssd, bsz, n_blk, reverse, tb):
    counts = tuple((len(gdn[k]), len(ssd[k])) for k in ('in_specs', 'out_shape', 'scratch'))
    res = pl.pallas_call(
        functools.partial(_scan_kernel, reverse, tb, n_blk, counts),
        out_shape=gdn['out_shape'] + ssd['out_shape'],
        grid=(bsz, n_blk),
        in_specs=gdn['in_specs'] + ssd['in_specs'],
        out_specs=gdn['out_specs'] + ssd['out_specs'],
        scratch_shapes=gdn['scratch'] + ssd['scratch'],
        compiler_params=_params("parallel", "arbitrary"),
        name="scan_bwd" if reverse else "scan_fwd",
    )(*gdn['args'], *ssd['args'])
    n_g = len(gdn['out_shape'])
    return gdn['finish'](res[:n_g]), ssd['finish'](res[n_g:])


def _outproj_kernel(x_ref, mod_ref, of_ref, ob_ref, gate_ref, om_ref, yf_ref, yb_ref, z_ref,
                    gn_ref, sn_ref, wout_ref, gpost_ref, o_ref):
    og = of_ref[...] + ob_ref[...]
    parts = []
    for hd in range(GDN_HEADS):
        lo = hd * GDN_DV
        oh = og[:, lo:lo + GDN_DV]
        parts.append(oh * lax.rsqrt(jnp.mean(oh * oh, axis=-1, keepdims=True) + EPS))
    o_gdn = jnp.concatenate(parts, axis=-1) * gn_ref[...] * _silu(gate_ref[...])
    o_ssd = _rms((yf_ref[...] + yb_ref[...]) * _silu(z_ref[...]), sn_ref[...])
    mixed = jnp.dot(jnp.concatenate([o_gdn.astype(BF16), om_ref[...], o_ssd.astype(BF16)], axis=-1),
                    wout_ref[...], preferred_element_type=F32)
    g1 = mod_ref[:, 2 * D_MODEL:3 * D_MODEL]
    o_ref[...] = x_ref[...] + g1 * _rms(mixed, gpost_ref[...])


def _outproj(x, mod, per_batch_mod, o_f, o_b, gate, o_mla, y_f, y_b, z, lw, tm):
    bsz, length, _ = x.shape
    tok = lambda w: pl.BlockSpec((None, tm, w), lambda b, i: (b, i, 0))
    mod_map = (lambda b, i: (b, 0, 0)) if per_batch_mod else (lambda b, i: (0, 0, 0))
    return pl.pallas_call(
        _outproj_kernel,
        out_shape=jax.ShapeDtypeStruct((bsz, length, D_MODEL), F32),
        grid=(bsz, length // tm),
        in_specs=[tok(D_MODEL), pl.BlockSpec((None, 1, 6 * D_MODEL), mod_map),
                  tok(GDN_WIDTH), tok(GDN_WIDTH), tok(GDN_WIDTH), tok(MLA_WIDTH),
                  tok(SSD_INNER), tok(SSD_INNER), tok(SSD_INNER),
                  _const_spec((1, GDN_WIDTH)), _const_spec((1, SSD_INNER)),
                  _const_spec((D_MODEL, D_MODEL)), _const_spec((1, D_MODEL))],
        out_specs=tok(D_MODEL),
        compiler_params=_params("parallel", "parallel"),
        name="outproj",
    )(x, mod, o_f, o_b, gate, o_mla, y_f, y_b, z, lw['gdn_norm'], lw['ssd_norm'], lw['w_out'],
      lw['norm_mix_post'])


FFN_COLS = 256
FFN_ROWS = 1024


def _ffn_kernel(tm, n_blk, main_ref, prev_ref, next_ref, mod_ref, gpre_ref, wup_ref, cw_ref,
                wdn_ref, gpost_ref, o_ref, xe_ref, act_ref):
    i = pl.program_id(1)
    rows = tm + 2 * HALO
    xe_ref[0:HALO, :] = prev_ref[...]
    xe_ref[HALO:HALO + tm, :] = main_ref[...]
    xe_ref[HALO + tm:rows, :] = next_ref[...]
    sh2 = mod_ref[:, 3 * D_MODEL:4 * D_MODEL]
    sc2 = mod_ref[:, 4 * D_MODEL:5 * D_MODEL]
    g2 = mod_ref[:, 5 * D_MODEL:6 * D_MODEL]
    h = _rms(xe_ref[...], gpre_ref[...]) * (1.0 + sc2) + sh2
    r = lax.broadcasted_iota(jnp.int32, (rows, 1), 0)
    valid = ((r >= HALO) | (i > 0)) & ((r < HALO + tm) | (i < n_blk - 1))
    hb = jnp.where(valid, h, 0.0).astype(BF16)

    def conv3(up, col0):
        acc = None
        for j in range(FFN_CONV_K):
            shift = (FFN_CONV_K // 2 - j) % rows
            tap = up if shift == 0 else pltpu.roll(up, shift, 0)
            term = tap * cw_ref[j:j + 1, col0:col0 + FFN_COLS]
            acc = term if acc is None else acc + term
        return acc[HALO:HALO + tm, :]

    def up(cb):
        c0 = cb * FFN_COLS
        return (jnp.dot(hb, wup_ref[:, c0:c0 + FFN_COLS], preferred_element_type=F32),
                jnp.dot(hb, wup_ref[:, D_FF + c0:D_FF + c0 + FFN_COLS],
                        preferred_element_type=F32))

    n_slab = D_FF // FFN_COLS
    acc = None
    ahead = 2
    pending = {cb: up(cb) for cb in range(ahead)}
    for cb in range(n_slab):
        c0 = cb * FFN_COLS
        g_up, u_up = pending.pop(cb)
        if cb + ahead < n_slab:
            pending[cb + ahead] = up(cb + ahead)
        act_ref[:, c0:c0 + FFN_COLS] = (_silu(conv3(g_up, c0))
                                        * conv3(u_up, D_FF + c0)).astype(BF16)
    acc = jnp.dot(act_ref[...], wdn_ref[...], preferred_element_type=F32)
    o_ref[...] = main_ref[...] + g2 * _rms(acc, gpost_ref[...])


def _ffn(x, mod, per_batch_mod, lw, tm):
    bsz, length, _ = x.shape
    n_blk = length // tm
    per = tm // HALO
    n_halo = length // HALO
    mod_map = (lambda b, i: (b, 0, 0)) if per_batch_mod else (lambda b, i: (0, 0, 0))
    return pl.pallas_call(
        functools.partial(_ffn_kernel, tm, n_blk),
        out_shape=jax.ShapeDtypeStruct((bsz, length, D_MODEL), F32),
        grid=(bsz, n_blk),
        in_specs=[pl.BlockSpec((None, tm, D_MODEL), lambda b, i: (b, i, 0)),
                  pl.BlockSpec((None, HALO, D_MODEL),
                               lambda b, i: (b, jnp.maximum(i * per - 1, 0), 0)),
                  pl.BlockSpec((None, HALO, D_MODEL),
                               lambda b, i: (b, jnp.minimum((i + 1) * per, n_halo - 1), 0)),
                  pl.BlockSpec((None, 1, 6 * D_MODEL), mod_map),
                  _const_spec((1, D_MODEL)), _const_spec((D_MODEL, 2 * D_FF)),
                  _const_spec((SUBLANES, 2 * D_FF)), _const_spec((D_FF, D_MODEL)),
                  _const_spec((1, D_MODEL))],
        out_specs=pl.BlockSpec((None, tm, D_MODEL), lambda b, i: (b, i, 0)),
        scratch_shapes=[pltpu.VMEM((tm + 2 * HALO, D_MODEL), F32),
                        pltpu.VMEM((tm, D_FF), BF16)],
        compiler_params=_params("parallel", "parallel"),
        name="ffn",
    )(x, x, x, mod, lw['norm_ffn_pre'], lw['ffn_w_up'], lw['ffn_conv'], lw['ffn_w_down'],
      lw['norm_ffn_post'])


def _pad_rows(w, rows):
    return jnp.pad(w, ((0, rows - w.shape[0]), (0, 0)))


def _lane_row(values, offset):
    return jnp.pad(values.reshape(1, -1), ((0, 0), (offset, LANES - offset - values.size)))


def _layer_weights(l, p):
    o = IN_OFFSETS
    w_in = p['w_in'][l]
    col = lambda k: w_in[:, o[k]:o[k] + IN_SIZES[k]]
    small = jnp.concatenate([col(6), col(2), col(3), col(9)], axis=1)
    small = jnp.pad(small, ((0, 0), (0, LANES - small.shape[1])))
    w_in_p = jnp.concatenate([col(0), col(1), col(4), col(5), col(7), col(8), small], axis=1)

    head_pad = HEAD_BLOCK - MLA_NOPE - MLA_ROPE
    w_uq = p['mla_w_uq'][l].reshape(MLA_Q_LORA, MLA_HEADS, MLA_NOPE + MLA_ROPE)
    w_uq = jnp.pad(w_uq, ((0, 0), (0, 0), (0, head_pad))).reshape(MLA_Q_LORA, -1)
    w_ukv = p['mla_w_ukv'][l].reshape(MLA_KV_LORA, MLA_HEADS, MLA_NOPE + MLA_V)
    w_k = jnp.pad(w_ukv[:, :, :MLA_NOPE], ((0, 0), (0, 0), (0, HEAD_BLOCK - MLA_NOPE)))
    w_k = w_k.reshape(MLA_KV_LORA, -1)
    w_v = w_ukv[:, :, MLA_NOPE:].reshape(MLA_KV_LORA, -1)
    place = jnp.pad(jnp.eye(MLA_ROPE, dtype=F32), ((0, LANES - MLA_ROPE), (MLA_NOPE, head_pad)))
    p_k = jnp.tile(place, (1, MLA_HEADS))

    gdn_prow = jnp.concatenate([_lane_row(p['gdn_a_log'][l], SM_GA),
                                _lane_row(p['gdn_dt_bias'][l], SM_GA)], axis=0)
    ssd_prow = jnp.concatenate([_lane_row(p['ssd_a_log'][l], SM_DT),
                                _lane_row(p['ssd_dt_bias'][l], SM_DT)], axis=0)
    row = lambda v: v.reshape(1, -1)
    return {
        'norm_mix_pre': row(p['norm_mix_pre'][l]), 'norm_mix_post': row(p['norm_mix_post'][l]),
        'norm_ffn_pre': row(p['norm_ffn_pre'][l]), 'norm_ffn_post': row(p['norm_ffn_post'][l]),
        'w_in': w_in_p.astype(BF16),
        'mla_q_norm': row(p['mla_q_norm'][l]), 'w_uq': w_uq.astype(BF16),
        'mla_kv_norm': row(p['mla_kv_norm'][l]), 'w_k': w_k.astype(BF16),
        'w_vt': w_v.T.astype(BF16), 'p_k': p_k.astype(BF16),
        'gdn_conv': _pad_rows(p['gdn_conv'][l], SUBLANES), 'gdn_prow': _pad_rows(gdn_prow, SUBLANES),
        'gdn_norm': row(jnp.tile(p['gdn_norm'][l], GDN_HEADS)),
        'ssd_conv': _pad_rows(p['ssd_conv'][l], SUBLANES), 'ssd_conv_b': row(p['ssd_conv_b'][l]),
        'ssd_prow': _pad_rows(ssd_prow, SUBLANES),
        'ssd_d': row(jnp.repeat(p['ssd_d'][l], SSD_HEAD_DIM)),
        'ssd_norm': row(p['ssd_norm'][l]),
        'w_out': p['w_out'][l].astype(BF16),
        'ffn_w_up': p['ffn_w_up'][l].astype(BF16),
        'ffn_conv': _pad_rows(p['ffn_conv'][l], SUBLANES),
        'ffn_w_down': p['ffn_w_down'][l].astype(BF16),
    }


def _rope_table(n_tokens):
    f32 = np.float32
    t = np.arange(n_tokens, dtype=np.int32)
    r = (t // GRID_W).astype(f32)
    col = (t % GRID_W).astype(f32)
    nf = MLA_ROPE // 4
    half = MLA_ROPE // 2
    inv = np.power(f32(ROPE_THETA), -np.arange(nf, dtype=f32) / f32(nf)).astype(f32)
    ang = np.concatenate([r[:, None] * inv, col[:, None] * inv], axis=-1).astype(f32)
    cos, sin = np.cos(ang).astype(f32), np.sin(ang).astype(f32)
    zeros = np.zeros_like(sin)

    def lanes(first, x1, x2, fill):
        out = np.full((n_tokens, LANES), fill, f32)
        out[:, first:first + half] = x1
        out[:, first + half:first + 2 * half] = x2
        return out

    parts = []
    for first in (MLA_NOPE, SM_KR):
        parts += [lanes(first, cos, cos, 1.0), lanes(first, zeros, sin, 0.0),
                  lanes(first, -sin, zeros, 0.0)]
    return jnp.asarray(np.concatenate(parts, axis=-1))


def _trunk_layer(x, mod, per_batch_mod, lw, rope_tab, ctx, tm, tq):
    bsz, length, _ = x.shape
    tb = min(SCAN_BLOCK, length)
    (qkv_pre, gate, qm, ckv, kmat, vmat, z, xbc_pre, small) = _inproj(
        x, mod, per_batch_mod, lw, rope_tab, tm)
    if ctx is None:
        kmats, vmats = [kmat], [vmat]
        s_gdn = jnp.zeros((2, bsz, GDN_HEADS, GDN_DK, GDN_DV), F32)
        s_ssd = jnp.zeros((2, bsz, SSD_HEADS, SSD_HEAD_DIM, SSD_STATE), F32)
    else:
        ctx_ckv, ctx_kr, st_gdn, st_ssd = ctx
        kr_pad = jnp.pad(ctx_kr, ((0, 0), (0, 0), (SM_KR, LANES - SM_KR - MLA_ROPE)))
        k_ctx, v_ctx = _ctxkv(ctx_ckv, kr_pad, lw)
        kmats, vmats = [kmat, k_ctx], [vmat, v_ctx]
        s_gdn = jnp.moveaxis(st_gdn, 1, 0)
        s_ssd = jnp.moveaxis(st_ssd, 1, 0)
    n_blk = length // tb
    (o_f, sg_f, qkv), (y_f, ss_f, xbc) = _scan(
        _gdn(qkv_pre, small, lw['gdn_conv'], lw['gdn_prow'], s_gdn[0], False, tb),
        _ssd(xbc_pre, small, lw['ssd_conv'], lw['ssd_conv_b'], lw['ssd_prow'], lw['ssd_d'],
             s_ssd[0], False, tb),
        bsz, n_blk, False, tb)
    (o_b, sg_b), (y_b, ss_b) = _scan(
        _gdn(qkv, small, None, lw['gdn_prow'], s_gdn[1], True, tb),
        _ssd(xbc, small, None, None, lw['ssd_prow'], None, s_ssd[1], True, tb),
        bsz, n_blk, True, tb)
    o_mla = _attention(qm, kmats, vmats, tq)
    x = _outproj(x, mod, per_batch_mod, o_f, o_b, gate, o_mla, y_f, y_b, z, lw, tm)
    x = _ffn(x, mod, per_batch_mod, lw, min(FFN_ROWS, length))
    ctx_out = (ckv, small[:, :, SM_KR:SM_KR + MLA_ROPE], jnp.stack([sg_f, sg_b], axis=1),
               jnp.stack([ss_f, ss_b], axis=1))
    return x, ctx_out


def kernel(x_prompt, x_sample, cache_mla_ckv, cache_mla_krope, state_gdn, state_ssd, c, c_ctx, w_ada, b_ada, norm_mix_pre, norm_mix_post, norm_ffn_pre, norm_ffn_post, w_in, gdn_conv, gdn_a_log, gdn_dt_bias, gdn_norm, mla_q_norm, mla_w_uq, mla_kv_norm, mla_w_ukv, ssd_conv, ssd_conv_b, ssd_a_log, ssd_dt_bias, ssd_d, ssd_norm, w_out, ffn_w_up, ffn_conv, ffn_w_down):
    p = dict(norm_mix_pre=norm_mix_pre, norm_mix_post=norm_mix_post, norm_ffn_pre=norm_ffn_pre,
             norm_ffn_post=norm_ffn_post, w_in=w_in, gdn_conv=gdn_conv, gdn_a_log=gdn_a_log,
             gdn_dt_bias=gdn_dt_bias, gdn_norm=gdn_norm, mla_q_norm=mla_q_norm,
             mla_w_uq=mla_w_uq, mla_kv_norm=mla_kv_norm, mla_w_ukv=mla_w_ukv, ssd_conv=ssd_conv,
             ssd_conv_b=ssd_conv_b, ssd_a_log=ssd_a_log, ssd_dt_bias=ssd_dt_bias, ssd_d=ssd_d,
             ssd_norm=ssd_norm, w_out=w_out, ffn_w_up=ffn_w_up, ffn_conv=ffn_conv,
             ffn_w_down=ffn_w_down)
    dec_batch = x_sample.shape[0]
    cond8 = jnp.concatenate(
        [c, c_ctx[None, :], jnp.zeros((SUBLANES - dec_batch - 1, D_MODEL), F32)], axis=0)
    mod = _ada(cond8, w_ada, b_ada)
    rope_tab = _rope_table(x_sample.shape[1])

    y_prompt, y_sample = x_prompt, x_sample
    ckv_l, kr_l, sg_l, ss_l = [], [], [], []
    for l in range(DEPTH):
        lw = _layer_weights(l, p)
        mod_s = mod[l, 0:dec_batch].reshape(dec_batch, 1, 6 * D_MODEL)
        mod_p = mod[l, dec_batch:dec_batch + 1].reshape(1, 1, 6 * D_MODEL)
        y_prompt, (ckv, kr, sg, ss) = _trunk_layer(
            y_prompt, mod_p, False, lw, None, None, tm=256, tq=256)
        ckv_l.append(ckv)
        kr_l.append(kr)
        sg_l.append(sg)
        ss_l.append(ss)
        y_sample, _ = _trunk_layer(
            y_sample, mod_s, True, lw, rope_tab,
            (cache_mla_ckv[:, l], cache_mla_krope[:, l], state_gdn[:, l], state_ssd[:, l]),
            tm=512, tq=256)
    return (y_prompt, y_sample, jnp.stack(ckv_l, axis=1), jnp.stack(kr_l, axis=1),
            jnp.stack(sg_l, axis=1), jnp.stack(ss_l, axis=1))
```
